```python
import jax, jax.numpy as jnp
from jax import lax
import numpy as np

D_MODEL = 1024
BATCH = 32
SEQ = 2048
DEPTH = 1

N_META = 16
GRID_W = 64
HEAD_DIM = 128
N_Q_HEADS = 8
N_KV_HEADS = 2
Q_GROUP = N_Q_HEADS // N_KV_HEADS
Q_WIDTH = N_Q_HEADS * HEAD_DIM
KV_WIDTH = N_KV_HEADS * HEAD_DIM
RNN_WIDTH = D_MODEL
RNN_BLOCKS = 8
RNN_BLOCK = RNN_WIDTH // RNN_BLOCKS
CONV_W = 4
CONV_PAD_L = CONV_W // 2
CONV_PAD_R = CONV_W - 1 - CONV_PAD_L
RG_C = 8.0
ROPE_THETA = 10000.0
ROPE_AXIS_DIM = HEAD_DIM // 2
ROPE_PAIRS = ROPE_AXIS_DIM // 2
Q_BLOCK = 128
D_FF = ((8 * D_MODEL + 3 * 256 - 1) // (3 * 256)) * 256
IN_WIDTH = Q_WIDTH + 2 * KV_WIDTH + 2 * RNN_WIDTH + 2 * D_MODEL
EPS = 1e-6

kernel_name = 'hybrid_rglru_axial_gqa_encoder_block'


def rmsnorm(x, g):
    xf = x.astype(jnp.float32)
    y = xf * lax.rsqrt(jnp.mean(xf * xf, axis=-1, keepdims=True) + EPS)
    return (y * g.astype(jnp.float32)).astype(x.dtype)


def axial_rope_tables(n_tok):
    rows = n_tok // GRID_W
    row = jnp.repeat(jnp.arange(rows, dtype=jnp.float32), GRID_W)
    col = jnp.tile(jnp.arange(GRID_W, dtype=jnp.float32), rows)
    zeros = jnp.zeros((N_META,), jnp.float32)
    row = jnp.concatenate([zeros, row])
    col = jnp.concatenate([zeros, col])
    inv_freq = jnp.exp(-jnp.log(jnp.float32(ROPE_THETA)) * jnp.arange(ROPE_PAIRS, dtype=jnp.float32) / ROPE_PAIRS)
    ang_r = row[:, None] * inv_freq[None, :]
    ang_c = col[:, None] * inv_freq[None, :]
    return jnp.cos(ang_r), jnp.sin(ang_r), jnp.cos(ang_c), jnp.sin(ang_c)


def rotate_half_axis(xh, cos, sin):
    c = cos[None, :, None, :]
    s = sin[None, :, None, :]
    x1, x2 = xh[..., :ROPE_PAIRS], xh[..., ROPE_PAIRS:]
    return jnp.concatenate([x1 * c - x2 * s, x2 * c + x1 * s], axis=-1)


def apply_axial_rope(x, tabs):
    cos_r, sin_r, cos_c, sin_c = tabs
    xf = x.astype(jnp.float32)
    xr = rotate_half_axis(xf[..., :ROPE_AXIS_DIM], cos_r, sin_r)
    xc = rotate_half_axis(xf[..., ROPE_AXIS_DIM:], cos_c, sin_c)
    return jnp.concatenate([xr, xc], axis=-1).astype(x.dtype)


def bidirectional_gqa(q, k, v):
    B, T = q.shape[0], q.shape[1]
    q = q.reshape(B, T, N_KV_HEADS, Q_GROUP, HEAD_DIM)
    scale = HEAD_DIM ** -0.5

    def block(qb):
        s = jnp.einsum('bqhgd,bkhd->bhgqk', qb, k, preferred_element_type=jnp.float32) * scale
        p = jax.nn.softmax(s, axis=-1)
        return jnp.einsum('bhgqk,bkhd->bqhgd', p.astype(v.dtype), v)

    q_meta, q_real = q[:, :N_META], q[:, N_META:]
    n_blk = q_real.shape[1] // Q_BLOCK
    qr = q_real.reshape(B, n_blk, Q_BLOCK, N_KV_HEADS, Q_GROUP, HEAD_DIM).transpose(1, 0, 2, 3, 4, 5)
    o_real = lax.map(block, qr)
    o_real = o_real.transpose(1, 0, 2, 3, 4, 5).reshape(B, n_blk * Q_BLOCK, Q_WIDTH)
    o_meta = block(q_meta).reshape(B, N_META, Q_WIDTH)
    return jnp.concatenate([o_meta, o_real], axis=1)


def centred_dwconv(x, w, b):
    T = x.shape[1]
    xp = jnp.pad(x, ((0, 0), (CONV_PAD_L, CONV_PAD_R), (0, 0)))
    y = b
    for j in range(CONV_W):
        y = y + xp[:, j:j + T] * w[j]
    return y


def _linear_recurrence_combine(e1, e2):
    a1, b1 = e1
    a2, b2 = e2
    return a1 * a2, a2 * b1 + b2


def rg_lru(xc, wa, ba, wx, bx, lam, reverse):
    B, T, C = xc.shape
    xf = xc.astype(jnp.float32)
    xb = xf.reshape(B, T, RNN_BLOCKS, RNN_BLOCK)
    r = jax.nn.sigmoid(jnp.einsum('btnc,ncd->btnd', xb, wa.astype(jnp.float32)).reshape(B, T, C) + ba.astype(jnp.float32))
    i = jax.nn.sigmoid(jnp.einsum('btnc,ncd->btnd', xb, wx.astype(jnp.float32)).reshape(B, T, C) + bx.astype(jnp.float32))
    log_a = RG_C * r * jax.nn.log_sigmoid(lam.astype(jnp.float32))
    a = jnp.exp(log_a)
    u = jnp.sqrt(-jnp.expm1(2.0 * log_a)) * (i * xf)
    _, h = lax.associative_scan(_linear_recurrence_combine, (a, u), axis=1, reverse=reverse)
    return h


def setup_inputs(seed: int = 0) -> dict:
    key = jax.random.key(seed)
    ks = jax.random.split(key, 20)
    f32 = jnp.float32

    def nrm(k, shape, fan_in):
        return jax.random.normal(k, shape, f32) * (fan_in ** -0.5)

    x = jax.random.normal(ks[0], (BATCH, SEQ, D_MODEL), f32)
    meta_tokens = jax.random.normal(ks[1], (N_META, D_MODEL), f32)
    norm1_g = 1.0 + 0.05 * jax.random.normal(ks[2], (DEPTH, D_MODEL), f32)
    w_in = nrm(ks[3], (DEPTH, D_MODEL, IN_WIDTH), D_MODEL)
    conv_w = nrm(ks[4], (DEPTH, CONV_W, RNN_WIDTH), CONV_W)
    conv_b = 0.02 * jax.random.normal(ks[5], (DEPTH, RNN_WIDTH), f32)
    rg_wa = nrm(ks[6], (DEPTH, 2, RNN_BLOCKS, RNN_BLOCK, RNN_BLOCK), RNN_BLOCK)
    rg_ba = 0.02 * jax.random.normal(ks[7], (DEPTH, 2, RNN_WIDTH), f32)
    rg_wx = nrm(ks[8], (DEPTH, 2, RNN_BLOCKS, RNN_BLOCK, RNN_BLOCK), RNN_BLOCK)
    rg_bx = 0.02 * jax.random.normal(ks[9], (DEPTH, 2, RNN_WIDTH), f32)
    a_c = jax.random.uniform(ks[10], (DEPTH, 2, RNN_WIDTH), f32, 0.9, 0.999)
    s = a_c ** (1.0 / RG_C)
    rg_lambda = jnp.log(s) - jnp.log1p(-s)
    q_norm_g = 1.0 + 0.05 * jax.random.normal(ks[11], (DEPTH, HEAD_DIM), f32)
    k_norm_g = 1.0 + 0.05 * jax.random.normal(ks[12], (DEPTH, HEAD_DIM), f32)
    w_out = nrm(ks[13], (DEPTH, D_MODEL, D_MODEL), D_MODEL)
    norm2_g = 1.0 + 0.05 * jax.random.normal(ks[14], (DEPTH, D_MODEL), f32)
    w_ffn_in = nrm(ks[15], (DEPTH, D_MODEL, 2 * D_FF), D_MODEL)
    w_ffn_out = nrm(ks[16], (DEPTH, D_FF, D_MODEL), D_FF)
    return {'x': x, 'meta_tokens': meta_tokens, 'norm1_g': norm1_g, 'w_in': w_in,
            'conv_w': conv_w, 'conv_b': conv_b, 'rg_wa': rg_wa, 'rg_ba': rg_ba,
            'rg_wx': rg_wx, 'rg_bx': rg_bx, 'rg_lambda': rg_lambda,
            'q_norm_g': q_norm_g, 'k_norm_g': k_norm_g, 'w_out': w_out,
            'norm2_g': norm2_g, 'w_ffn_in': w_ffn_in, 'w_ffn_out': w_ffn_out}


def reference(x, meta_tokens, norm1_g, w_in, conv_w, conv_b, rg_wa, rg_ba, rg_wx, rg_bx,
              rg_lambda, q_norm_g, k_norm_g, w_out, norm2_g, w_ffn_in, w_ffn_out):
    B, S, D = x.shape
    meta = jnp.broadcast_to(meta_tokens.astype(x.dtype)[None], (B, N_META, D))
    h = jnp.concatenate([meta, x], axis=1)
    T = h.shape[1]
    tabs = axial_rope_tables(S)
    splits = [Q_WIDTH, Q_WIDTH + KV_WIDTH, Q_WIDTH + 2 * KV_WIDTH,
              Q_WIDTH + 2 * KV_WIDTH + RNN_WIDTH, Q_WIDTH + 2 * KV_WIDTH + 2 * RNN_WIDTH]

    for l in range(DEPTH):
        xn = rmsnorm(h, norm1_g[l])
        proj = xn @ w_in[l]
        q, k, v, xr, gr, g_merge = jnp.split(proj, splits, axis=-1)

        q = q.reshape(B, T, N_Q_HEADS, HEAD_DIM)
        k = k.reshape(B, T, N_KV_HEADS, HEAD_DIM)
        v = v.reshape(B, T, N_KV_HEADS, HEAD_DIM)
        q = apply_axial_rope(rmsnorm(q, q_norm_g[l]), tabs)
        k = apply_axial_rope(rmsnorm(k, k_norm_g[l]), tabs)
        attn = bidirectional_gqa(q, k, v)

        xc = centred_dwconv(xr, conv_w[l], conv_b[l])
        rnn = (rg_lru(xc, rg_wa[l, 0], rg_ba[l, 0], rg_wx[l, 0], rg_bx[l, 0], rg_lambda[l, 0], False)
               + rg_lru(xc, rg_wa[l, 1], rg_ba[l, 1], rg_wx[l, 1], rg_bx[l, 1], rg_lambda[l, 1], True))
        rnn = rnn.astype(x.dtype) * jax.nn.gelu(gr)

        gates = jax.nn.sigmoid(g_merge)
        g_attn, g_rnn = gates[..., :D_MODEL], gates[..., D_MODEL:]
        mix = g_attn * attn + g_rnn * rnn
        h = h + mix @ w_out[l]

        hn = rmsnorm(h, norm2_g[l])
        gu = hn @ w_ffn_in[l]
        g, u = gu[..., :D_FF], gu[..., D_FF:]
        h = h + (jax.nn.silu(g) * u) @ w_ffn_out[l]

    return h[:, N_META:]
```

```python
import functools

import jax
import jax.numpy as jnp
from jax import lax
from jax.experimental import pallas as pl
from jax.experimental.pallas import tpu as pltpu

N_META = 16
GRID_W = 64
HEAD_DIM = 128
N_Q_HEADS = 8
N_KV_HEADS = 2
Q_GROUP = N_Q_HEADS // N_KV_HEADS
RNN_BLOCK = 128
CONV_W = 4
CONV_PAD_L = CONV_W // 2
RG_C = 8.0
ROPE_THETA = 10000.0
ROPE_PAIRS = HEAD_DIM // 4
EPS = 1e-6

LANES = 128
SUBLANES = 8
VMEM_LIMIT_BYTES = 56 * 1024 * 1024

BF16 = jnp.bfloat16
F32 = jnp.float32


def _resident(shape):
    nd = len(shape)
    return pl.BlockSpec(shape, lambda *_: (0,) * nd, pipeline_mode=pl.Buffered(1))


def _swap_halves(x):
    lane = lax.broadcasted_iota(jnp.int32, x.shape, 1)
    lo = (lane % (2 * ROPE_PAIRS)) < ROPE_PAIRS
    return jnp.where(lo, pltpu.roll(x, LANES - ROPE_PAIRS, 1), pltpu.roll(x, ROPE_PAIRS, 1))


def _head_norm_rope(xh, g, cos, sin):
    ms = jnp.mean(xh * xh, axis=-1, keepdims=True)
    xn = xh * lax.rsqrt(ms + EPS) * g
    return xn * cos + _swap_halves(xn) * sin


def _proj_kernel(x_ref, g1_ref, w_ref, qg_ref, kg_ref, cos_ref, sin_ref,
                 q_ref, k_ref, vt_ref, xr_ref, gr_ref, gm_ref, *, d_model, q_width, kv_width, rnn_width):
    x = x_ref[...]
    ms = jnp.mean(x * x, axis=-1, keepdims=True)
    xn = (x * lax.rsqrt(ms + EPS) * g1_ref[...]).astype(BF16)

    cos = cos_ref[...]
    sin = sin_ref[...]
    scale = HEAD_DIM ** -0.5

    o_k = q_width
    o_v = o_k + kv_width
    o_xr = o_v + kv_width
    o_gr = o_xr + rnn_width
    o_gm = o_gr + rnn_width

    pq = jnp.dot(xn, w_ref[:, 0:o_k], preferred_element_type=F32)
    for h in range(q_width // HEAD_DIM):
        sl = slice(h * HEAD_DIM, (h + 1) * HEAD_DIM)
        qh = _head_norm_rope(pq[:, sl], qg_ref[...], cos, sin)
        q_ref[:, sl] = (qh * scale).astype(BF16)

    pkv = jnp.dot(xn, w_ref[:, o_k:o_xr], preferred_element_type=F32)
    for h in range(kv_width // HEAD_DIM):
        sl = slice(h * HEAD_DIM, (h + 1) * HEAD_DIM)
        kh = _head_norm_rope(pkv[:, sl], kg_ref[...], cos, sin)
        k_ref[:, sl] = kh.astype(BF16)
    vt_ref[...] = pkv[:, kv_width:].T.astype(BF16)

    pxr = jnp.dot(xn, w_ref[:, o_xr:o_gr], preferred_element_type=F32)
    for n in range(rnn_width // RNN_BLOCK):
        xr_ref[n] = pxr[:, n * RNN_BLOCK:(n + 1) * RNN_BLOCK]

    gr_ref[...] = jnp.dot(xn, w_ref[:, o_gr:o_gm], preferred_element_type=F32)
    gm_ref[...] = jnp.dot(xn, w_ref[:, o_gm:], preferred_element_type=F32)


def _proj(x2d, g1, w_in, qg, kg, cos, sin, *, tm, q_width, kv_width, rnn_width):
    m, d = x2d.shape
    in_width = w_in.shape[1]
    gm_width = in_width - q_width - 2 * kv_width - 2 * rnn_width
    n_tab = cos.shape[0] // tm
    n_slab = rnn_width // RNN_BLOCK
    kern = functools.partial(_proj_kernel, d_model=d, q_width=q_width, kv_width=kv_width,
                             rnn_width=rnn_width)
    return pl.pallas_call(
        kern,
        grid=(m // tm,),
        in_specs=[
            pl.BlockSpec((tm, d), lambda i: (i, 0)),
            _resident((1, d)),
            _resident((d, in_width)),
            _resident((1, HEAD_DIM)),
            _resident((1, HEAD_DIM)),
            pl.BlockSpec((tm, HEAD_DIM), lambda i: (i % n_tab, 0)),
            pl.BlockSpec((tm, HEAD_DIM), lambda i: (i % n_tab, 0)),
        ],
        out_specs=[
            pl.BlockSpec((tm, q_width), lambda i: (i, 0)),
            pl.BlockSpec((tm, kv_width), lambda i: (i, 0)),
            pl.BlockSpec((kv_width, tm), lambda i: (0, i)),
            pl.BlockSpec((n_slab, tm, RNN_BLOCK), lambda i: (0, i, 0)),
            pl.BlockSpec((tm, rnn_width), lambda i: (i, 0)),
            pl.BlockSpec((tm, gm_width), lambda i: (i, 0)),
        ],
        out_shape=[
            jax.ShapeDtypeStruct((m, q_width), BF16),
            jax.ShapeDtypeStruct((m, kv_width), BF16),
            jax.ShapeDtypeStruct((kv_width, m), BF16),
            jax.ShapeDtypeStruct((n_slab, m, RNN_BLOCK), F32),
            jax.ShapeDtypeStruct((m, rnn_width), F32),
            jax.ShapeDtypeStruct((m, gm_width), F32),
        ],
        compiler_params=pltpu.CompilerParams(
            dimension_semantics=("arbitrary",), vmem_limit_bytes=VMEM_LIMIT_BYTES),
        name="proj",
    )(x2d, g1, w_in, qg, kg, cos, sin)


def _attn_kernel(q_ref, k_ref, vt_ref, km_ref, vtm_ref, o_ref, *, bq):
    q = q_ref[...]
    qs = jnp.concatenate([q[:, g * HEAD_DIM:(g + 1) * HEAD_DIM] for g in range(Q_GROUP)], axis=0)
    nt = (((1,), (1,)), ((), ()))
    st = lax.dot_general(k_ref[...], qs, nt, preferred_element_type=F32)
    stm = lax.dot_general(km_ref[...], qs, nt, preferred_element_type=F32)
    m = jnp.maximum(jnp.max(st, axis=0, keepdims=True), jnp.max(stm, axis=0, keepdims=True))
    p = jnp.exp(st - m)
    pm = jnp.exp(stm - m)
    l = jnp.sum(p, axis=0, keepdims=True) + jnp.sum(pm, axis=0, keepdims=True)
    ot = jnp.dot(vt_ref[...], p.astype(BF16), preferred_element_type=F32)
    ot = ot + jnp.dot(vtm_ref[...], pm.astype(BF16), preferred_element_type=F32)
    ot = ot * (1.0 / l)
    for g in range(Q_GROUP):
        o_ref[:, g * HEAD_DIM:(g + 1) * HEAD_DIM] = ot[:, g * bq:(g + 1) * bq].T.astype(BF16)


def _attn(q, k, vt, k_meta, vt_meta, *, batch, seq, bq):
    m = q.shape[0]
    nq = seq // bq
    gw = Q_GROUP * HEAD_DIM
    kern = functools.partial(_attn_kernel, bq=bq)
    return pl.pallas_call(
        kern,
        grid=(batch, N_KV_HEADS, nq),
        in_specs=[
            pl.BlockSpec((bq, gw), lambda b, h, i: (b * nq + i, h)),
            pl.BlockSpec((seq, HEAD_DIM), lambda b, h, i: (b, h)),
            pl.BlockSpec((HEAD_DIM, seq), lambda b, h, i: (h, b)),
            pl.BlockSpec((N_META, HEAD_DIM), lambda b, h, i: (0, h)),
            pl.BlockSpec((HEAD_DIM, N_META), lambda b, h, i: (h, 0)),
        ],
        out_specs=pl.BlockSpec((bq, gw), lambda b, h, i: (b * nq + i, h)),
        out_shape=jax.ShapeDtypeStruct((m, N_Q_HEADS * HEAD_DIM), BF16),
        compiler_params=pltpu.CompilerParams(
            dimension_semantics=("arbitrary", "arbitrary", "arbitrary"),
            vmem_limit_bytes=VMEM_LIMIT_BYTES),
        name="attn",
    )(q, k, vt, k_meta, vt_meta)


SCAN_SEGMENTS = SUBLANES


def _rnn_kernel(xr_ref, xm_ref, cw_ref, cb_ref, w4_ref, b4_ref, lam_ref, o_ref,
                xs_ref, a_ref, u_ref, hl_ref, p_ref, acc_ref, *, seq):
    t_all = N_META + seq
    seg = t_all // SCAN_SEGMENTS
    pad = SUBLANES

    xs_ref[0:pad, :] = jnp.zeros((pad, LANES), F32)
    xs_ref[pad:pad + N_META, :] = xm_ref[0]
    xs_ref[pad + N_META:pad + t_all, :] = xr_ref[0]
    xs_ref[pad + t_all:pad + t_all + pad, :] = jnp.zeros((pad, LANES), F32)

    xc = cb_ref[...]
    for j in range(CONV_W):
        off = pad + j - CONV_PAD_L
        xc = xc + xs_ref[off:off + t_all, :] * cw_ref[j:j + 1, :]

    gates = jnp.dot(xc.astype(BF16), w4_ref[0], preferred_element_type=F32) + b4_ref[...]

    def run_direction(d, reverse):
        r = jax.nn.sigmoid(gates[:, (2 * d) * LANES:(2 * d + 1) * LANES])
        i = jax.nn.sigmoid(gates[:, (2 * d + 1) * LANES:(2 * d + 2) * LANES])
        log_a = RG_C * r * jax.nn.log_sigmoid(lam_ref[d:d + 1, :])
        a = jnp.exp(log_a)
        a_ref[...] = a
        u_ref[...] = jnp.sqrt(jnp.tanh(-log_a) * (1.0 + a * a)) * (i * xc)

        def step(jj, carry):
            h, p = carry
            j = (seg - 1 - jj) if reverse else jj
            rows = pl.ds(j, SCAN_SEGMENTS, stride=seg)
            av = a_ref[rows, :]
            h = av * h + u_ref[rows, :]
            p = av * p
            hl_ref[rows, :] = h
            p_ref[rows, :] = p
            return h, p

        h_end, p_end = lax.fori_loop(
            0, seg, step, (jnp.zeros((SCAN_SEGMENTS, LANES), F32), jnp.ones((SCAN_SEGMENTS, LANES), F32)))

        order = range(SCAN_SEGMENTS - 1, -1, -1) if reverse else range(SCAN_SEGMENTS)
        c = jnp.zeros((1, LANES), F32)
        cs = [None] * SCAN_SEGMENTS
        for s in order:
            cs[s] = c
            c = p_end[s:s + 1, :] * c + h_end[s:s + 1, :]
        c_in = jnp.concatenate(cs, axis=0)

        def fix(j, _):
            rows = pl.ds(j, SCAN_SEGMENTS, stride=seg)
            h = hl_ref[rows, :] + p_ref[rows, :] * c_in
            if reverse:
                h = h + acc_ref[rows, :]
            acc_ref[rows, :] = h
            return 0

        lax.fori_loop(0, seg, fix, 0)

    run_direction(0, False)
    run_direction(1, True)
    o_ref[...] = acc_ref[N_META:, :]


def _rnn(xr, xr_meta, conv_w, conv_b, w4, b4, lam, *, batch, seq):
    n_slab, m, _ = xr.shape
    t_all = N_META + seq
    kern = functools.partial(_rnn_kernel, seq=seq)
    return pl.pallas_call(
        kern,
        grid=(batch, n_slab),
        in_specs=[
            pl.BlockSpec((1, seq, LANES), lambda b, n: (n, b, 0)),
            pl.BlockSpec((1, N_META, LANES), lambda b, n: (n, 0, 0)),
            pl.BlockSpec((CONV_W, LANES), lambda b, n: (0, n)),
            pl.BlockSpec((1, LANES), lambda b, n: (0, n)),
            pl.BlockSpec((1, LANES, 4 * LANES), lambda b, n: (n, 0, 0)),
            pl.BlockSpec((1, 4 * LANES), lambda b, n: (0, n)),
            pl.BlockSpec((2, LANES), lambda b, n: (0, n)),
        ],
        out_specs=pl.BlockSpec((seq, LANES), lambda b, n: (b, n)),
        out_shape=jax.ShapeDtypeStruct((m, n_slab * LANES), F32),
        scratch_shapes=[
            pltpu.VMEM((t_all + 2 * SUBLANES, LANES), F32),
            pltpu.VMEM((t_all, LANES), F32),
            pltpu.VMEM((t_all, LANES), F32),
            pltpu.VMEM((t_all, LANES), F32),
            pltpu.VMEM((t_all, LANES), F32),
            pltpu.VMEM((t_all, LANES), F32),
        ],
        compiler_params=pltpu.CompilerParams(
            dimension_semantics=("arbitrary", "arbitrary"), vmem_limit_bytes=VMEM_LIMIT_BYTES),
        name="rnn",
    )(xr, xr_meta, conv_w, conv_b, w4, b4, lam)


def _post_kernel(x_ref, attn_ref, rnn_ref, gr_ref, gm_ref, wo_ref, g2_ref, wi_ref, wf_ref, o_ref,
                 *, d_model, d_ff):
    gates = jax.nn.sigmoid(gm_ref[...])
    rnn = rnn_ref[...] * jax.nn.gelu(gr_ref[...])
    mix = gates[:, :d_model] * attn_ref[...].astype(F32) + gates[:, d_model:] * rnn
    h1 = x_ref[...] + jnp.dot(mix.astype(BF16), wo_ref[...], preferred_element_type=F32)
    ms = jnp.mean(h1 * h1, axis=-1, keepdims=True)
    hn = (h1 * lax.rsqrt(ms + EPS) * g2_ref[...]).astype(BF16)
    gu = jnp.dot(hn, wi_ref[...], preferred_element_type=F32)
    act = (jax.nn.silu(gu[:, :d_ff]) * gu[:, d_ff:]).astype(BF16)
    o_ref[...] = h1 + jnp.dot(act, wf_ref[...], preferred_element_type=F32)


def _post(x2d, attn, rnn, gr, gm, w_out, g2, w_ffn_in, w_ffn_out, *, tm):
    m, d = x2d.shape
    d_ff = w_ffn_out.shape[0]
    kern = functools.partial(_post_kernel, d_model=d, d_ff=d_ff)
    row = lambda w: pl.BlockSpec((tm, w), lambda i: (i, 0))
    return pl.pallas_call(
        kern,
        grid=(m // tm,),
        in_specs=[row(d), row(d), row(d), row(d), row(2 * d),
                  _resident((d, d)), _resident((1, d)), _resident((d, 2 * d_ff)), _resident((d_ff, d))],
        out_specs=row(d),
        out_shape=jax.ShapeDtypeStruct((m, d), F32),
        compiler_params=pltpu.CompilerParams(
            dimension_semantics=("arbitrary",), vmem_limit_bytes=VMEM_LIMIT_BYTES),
        name="post",
    )(x2d, attn, rnn, gr, gm, w_out, g2, w_ffn_in, w_ffn_out)


def _rope_tables(seq):
    pos = jnp.arange(seq, dtype=jnp.int32)
    row = (pos // GRID_W).astype(F32)
    col = (pos % GRID_W).astype(F32)
    inv_freq = jnp.exp(-jnp.log(jnp.float32(ROPE_THETA)) * jnp.arange(ROPE_PAIRS, dtype=F32) / ROPE_PAIRS)
    ang_r = row[:, None] * inv_freq[None, :]
    ang_c = col[:, None] * inv_freq[None, :]
    cos = jnp.concatenate([jnp.cos(ang_r)] * 2 + [jnp.cos(ang_c)] * 2, axis=-1)
    sin = jnp.concatenate([-jnp.sin(ang_r), jnp.sin(ang_r), -jnp.sin(ang_c), jnp.sin(ang_c)], axis=-1)
    return cos, sin


def kernel(x, meta_tokens, norm1_g, w_in, conv_w, conv_b, rg_wa, rg_ba, rg_wx, rg_bx, rg_lambda,
           q_norm_g, k_norm_g, w_out, norm2_g, w_ffn_in, w_ffn_out):
    batch, seq, d = x.shape
    depth = norm1_g.shape[0]
    assert depth == 1, "meta-token rows are only skipped because no later layer reads them"
    q_width = N_Q_HEADS * HEAD_DIM
    kv_width = N_KV_HEADS * HEAD_DIM
    rnn_width = conv_w.shape[-1]
    n_slab = rnn_width // RNN_BLOCK
    assert meta_tokens.shape == (N_META, d) and seq % GRID_W == 0

    x2d = x.reshape(batch * seq, d)
    g1 = norm1_g[0].reshape(1, d)
    w_in_b = w_in[0].astype(BF16)
    qg = q_norm_g[0].reshape(1, HEAD_DIM)
    kg = k_norm_g[0].reshape(1, HEAD_DIM)
    cos, sin = _rope_tables(seq)
    cos_meta = jnp.ones((N_META, HEAD_DIM), F32)
    sin_meta = jnp.zeros((N_META, HEAD_DIM), F32)

    proj = functools.partial(_proj, q_width=q_width, kv_width=kv_width, rnn_width=rnn_width)
    q, k, vt, xr, gr, gm = proj(x2d, g1, w_in_b, qg, kg, cos, sin, tm=256)
    _, k_meta, vt_meta, xr_meta, _, _ = proj(
        meta_tokens.astype(x.dtype), g1, w_in_b, qg, kg, cos_meta, sin_meta, tm=N_META)

    attn = _attn(q, k, vt, k_meta, vt_meta, batch=batch, seq=seq, bq=256)

    w4 = jnp.concatenate([rg_wa[0, 0], rg_wx[0, 0], rg_wa[0, 1], rg_wx[0, 1]], axis=-1).astype(BF16)
    b4 = jnp.stack([rg_ba[0, 0], rg_bx[0, 0], rg_ba[0, 1], rg_bx[0, 1]], axis=0)
    b4 = b4.reshape(4, n_slab, RNN_BLOCK).transpose(1, 0, 2).reshape(1, n_slab * 4 * RNN_BLOCK)
    rnn = _rnn(xr, xr_meta, conv_w[0], conv_b[0].reshape(1, rnn_width), w4, b4, rg_lambda[0],
               batch=batch, seq=seq)

    out = _post(x2d, attn, rnn, gr, gm, w_out[0].astype(BF16), norm2_g[0].reshape(1, d),
                w_ffn_in[0].astype(BF16), w_ffn_out[0].astype(BF16), tm=256)
    return out.reshape(batch, seq, d)
```

```python
import functools

import jax
import jax.numpy as jnp
from jax import lax
from jax.experimental import pallas as pl
from jax.experimental.pallas import tpu as pltpu

N_META = 16
GRID_W = 64
HEAD_DIM = 128
N_Q_HEADS = 8
N_KV_HEADS = 2
Q_GROUP = N_Q_HEADS // N_KV_HEADS
RNN_BLOCK = 128
CONV_W = 4
CONV_PAD_L = CONV_W // 2
RG_C = 8.0
ROPE_THETA = 10000.0
ROPE_PAIRS = HEAD_DIM // 4
EPS = 1e-6
LOG2_E = 1.4426950408889634
F32_TINY = 1.1754943508222875e-38

LANES = 128
SUBLANES = 8
VMEM_LIMIT_BYTES = 56 * 1024 * 1024

BF16 = jnp.bfloat16
F32 = jnp.float32


def _resident(shape):
    nd = len(shape)
    return pl.BlockSpec(shape, lambda *_: (0,) * nd, pipeline_mode=pl.Buffered(1))


def _swap_halves(x):
    lane = lax.broadcasted_iota(jnp.int32, x.shape, 1)
    lo = (lane % (2 * ROPE_PAIRS)) < ROPE_PAIRS
    return jnp.where(lo, pltpu.roll(x, LANES - ROPE_PAIRS, 1), pltpu.roll(x, ROPE_PAIRS, 1))


def _head_norm_rope(xh, g, cos, sin):
    ms = jnp.mean(xh * xh, axis=-1, keepdims=True)
    xn = xh * lax.rsqrt(ms + EPS) * g
    return xn * cos + _swap_halves(xn) * sin


def _proj_kernel(x_ref, g1_ref, w_ref, qg_ref, kg_ref, cos_ref, sin_ref,
                 q_ref, k_ref, vt_ref, xr_ref, gr_ref, gm_ref, *, d_model, q_width, kv_width, rnn_width):
    x = x_ref[...]
    ms = jnp.mean(x * x, axis=-1, keepdims=True)
    xn = (x * lax.rsqrt(ms + EPS) * g1_ref[...]).astype(BF16)

    cos = cos_ref[...]
    sin = sin_ref[...]
    scale = HEAD_DIM ** -0.5 * LOG2_E

    o_k = q_width
    o_v = o_k + kv_width
    o_xr = o_v + kv_width
    o_gr = o_xr + rnn_width
    o_gm = o_gr + rnn_width

    pq = jnp.dot(xn, w_ref[:, 0:o_k], preferred_element_type=F32)
    for h in range(q_width // HEAD_DIM):
        sl = slice(h * HEAD_DIM, (h + 1) * HEAD_DIM)
        qh = _head_norm_rope(pq[:, sl], qg_ref[...], cos, sin)
        q_ref[:, sl] = (qh * scale).astype(BF16)

    pkv = jnp.dot(xn, w_ref[:, o_k:o_xr], preferred_element_type=F32)
    for h in range(kv_width // HEAD_DIM):
        sl = slice(h * HEAD_DIM, (h + 1) * HEAD_DIM)
        kh = _head_norm_rope(pkv[:, sl], kg_ref[...], cos, sin)
        k_ref[:, sl] = kh.astype(BF16)
    vt_ref[...] = pkv[:, kv_width:].T.astype(BF16)

    pxr = jnp.dot(xn, w_ref[:, o_xr:o_gr], preferred_element_type=F32)
    for n in range(rnn_width // RNN_BLOCK):
        xr_ref[n] = pxr[:, n * RNN_BLOCK:(n + 1) * RNN_BLOCK]

    gr_ref[...] = jnp.dot(xn, w_ref[:, o_gr:o_gm], preferred_element_type=F32)
    gm_ref[...] = jnp.dot(xn, w_ref[:, o_gm:], preferred_element_type=F32)


def _proj(x2d, g1, w_in, qg, kg, cos, sin, *, tm, q_width, kv_width, rnn_width):
    m, d = x2d.shape
    in_width = w_in.shape[1]
    gm_width = in_width - q_width - 2 * kv_width - 2 * rnn_width
    n_tab = cos.shape[0] // tm
    n_slab = rnn_width // RNN_BLOCK
    kern = functools.partial(_proj_kernel, d_model=d, q_width=q_width, kv_width=kv_width,
                             rnn_width=rnn_width)
    return pl.pallas_call(
        kern,
        grid=(m // tm,),
        in_specs=[
            pl.BlockSpec((tm, d), lambda i: (i, 0)),
            _resident((1, d)),
            _resident((d, in_width)),
            _resident((1, HEAD_DIM)),
            _resident((1, HEAD_DIM)),
            pl.BlockSpec((tm, HEAD_DIM), lambda i: (i % n_tab, 0)),
            pl.BlockSpec((tm, HEAD_DIM), lambda i: (i % n_tab, 0)),
        ],
        out_specs=[
            pl.BlockSpec((tm, q_width), lambda i: (i, 0)),
            pl.BlockSpec((tm, kv_width), lambda i: (i, 0)),
            pl.BlockSpec((kv_width, tm), lambda i: (0, i)),
            pl.BlockSpec((n_slab, tm, RNN_BLOCK), lambda i: (0, i, 0)),
            pl.BlockSpec((tm, rnn_width), lambda i: (i, 0)),
            pl.BlockSpec((tm, gm_width), lambda i: (i, 0)),
        ],
        out_shape=[
            jax.ShapeDtypeStruct((m, q_width), BF16),
            jax.ShapeDtypeStruct((m, kv_width), BF16),
            jax.ShapeDtypeStruct((kv_width, m), BF16),
            jax.ShapeDtypeStruct((n_slab, m, RNN_BLOCK), F32),
            jax.ShapeDtypeStruct((m, rnn_width), F32),
            jax.ShapeDtypeStruct((m, gm_width), F32),
        ],
        compiler_params=pltpu.CompilerParams(
            dimension_semantics=("arbitrary",), vmem_limit_bytes=VMEM_LIMIT_BYTES),
        name="proj",
    )(x2d, g1, w_in, qg, kg, cos, sin)


ATTN_KEY_CHUNK = 512


def _attn_kernel(q_ref, k_ref, vt_ref, km_ref, vtm_ref, o_ref, *, bq):
    q = q_ref[...]
    qs = jnp.concatenate([q[:, g * HEAD_DIM:(g + 1) * HEAD_DIM] for g in range(Q_GROUP)], axis=0)
    nt = (((1,), (1,)), ((), ()))
    n_chunks = k_ref.shape[0] // ATTN_KEY_CHUNK

    def scores(c):
        return lax.dot_general(k_ref[c * ATTN_KEY_CHUNK:(c + 1) * ATTN_KEY_CHUNK, :], qs, nt,
                               preferred_element_type=F32)

    stm = lax.dot_general(km_ref[...], qs, nt, preferred_element_type=F32)
    m = jnp.max(stm, axis=0, keepdims=True)
    pm = jnp.exp2(stm - m)
    l = jnp.sum(pm, axis=0, keepdims=True)
    ot = jnp.dot(vtm_ref[...], pm.astype(BF16), preferred_element_type=F32)
    st_next = scores(0)
    for c in range(n_chunks):
        st = st_next
        if c + 1 < n_chunks:
            st_next = scores(c + 1)
        m_new = jnp.maximum(m, jnp.max(st, axis=0, keepdims=True))
        alpha = jnp.exp2(m - m_new)
        p = jnp.exp2(st - m_new)
        l = alpha * l + jnp.sum(p, axis=0, keepdims=True)
        pv = jnp.dot(vt_ref[:, c * ATTN_KEY_CHUNK:(c + 1) * ATTN_KEY_CHUNK], p.astype(BF16),
                     preferred_element_type=F32)
        ot = alpha * ot + pv
        m = m_new
    ot = ot * (1.0 / l)
    for g in range(Q_GROUP):
        o_ref[:, g * HEAD_DIM:(g + 1) * HEAD_DIM] = ot[:, g * bq:(g + 1) * bq].T.astype(BF16)


def _attn(q, k, vt, k_meta, vt_meta, *, batch, seq, bq):
    m = q.shape[0]
    nq = seq // bq
    gw = Q_GROUP * HEAD_DIM
    kern = functools.partial(_attn_kernel, bq=bq)
    return pl.pallas_call(
        kern,
        grid=(batch, N_KV_HEADS, nq),
        in_specs=[
            pl.BlockSpec((bq, gw), lambda b, h, i: (b * nq + i, h)),
            pl.BlockSpec((seq, HEAD_DIM), lambda b, h, i: (b, h)),
            pl.BlockSpec((HEAD_DIM, seq), lambda b, h, i: (h, b)),
            pl.BlockSpec((N_META, HEAD_DIM), lambda b, h, i: (0, h)),
            pl.BlockSpec((HEAD_DIM, N_META), lambda b, h, i: (h, 0)),
        ],
        out_specs=pl.BlockSpec((bq, gw), lambda b, h, i: (b * nq + i, h)),
        out_shape=jax.ShapeDtypeStruct((m, N_Q_HEADS * HEAD_DIM), BF16),
        compiler_params=pltpu.CompilerParams(
            dimension_semantics=("arbitrary", "arbitrary", "arbitrary"),
            vmem_limit_bytes=VMEM_LIMIT_BYTES),
        name="attn",
    )(q, k, vt, k_meta, vt_meta)


SCAN_SEGMENTS = SUBLANES
SCAN_UNROLL = 6


def _rnn_kernel(xr_ref, xm_ref, cw_ref, cb_ref, w4_ref, b4_ref, lam_ref, o_ref,
                xs_ref, a0_ref, a1_ref, u0_ref, u1_ref, hl0_ref, hl1_ref, p0_ref, p1_ref, acc_ref,
                *, seq):
    a_refs, u_refs = (a0_ref, a1_ref), (u0_ref, u1_ref)
    hl_refs, p_refs = (hl0_ref, hl1_ref), (p0_ref, p1_ref)
    t_all = N_META + seq
    seg = t_all // SCAN_SEGMENTS
    pad = SUBLANES

    xs_ref[0:pad, :] = jnp.zeros((pad, LANES), F32)
    xs_ref[pad:pad + N_META, :] = xm_ref[0]
    xs_ref[pad + N_META:pad + t_all, :] = xr_ref[0]
    xs_ref[pad + t_all:pad + t_all + pad, :] = jnp.zeros((pad, LANES), F32)

    xc = cb_ref[...]
    for j in range(CONV_W):
        off = pad + j - CONV_PAD_L
        xc = xc + xs_ref[off:off + t_all, :] * cw_ref[j:j + 1, :]
    xh = 0.5 * xc

    zh = jnp.dot(xc.astype(BF16), w4_ref[0], preferred_element_type=F32) + b4_ref[...]

    for d in range(2):
        t_r = jnp.tanh(zh[:, (2 * d) * LANES:(2 * d + 1) * LANES])
        t_i = jnp.tanh(zh[:, (2 * d + 1) * LANES:(2 * d + 2) * LANES])
        c_half = (0.5 * RG_C) * jax.nn.log_sigmoid(lam_ref[d:d + 1, :])
        log_a = c_half * t_r + c_half
        a = jnp.exp(log_a)
        a_refs[d][...] = a
        y = jnp.tanh(-log_a) * (a * a + 1.0)
        root = y * lax.rsqrt(jnp.maximum(y, F32_TINY))
        u_refs[d][...] = root * (t_i * xh + xh)

    def step(jj, carry):
        new = []
        for d in range(2):
            h, p = carry[d]
            j = jj if d == 0 else seg - 1 - jj
            rows = pl.ds(j, SCAN_SEGMENTS, stride=seg)
            av = a_refs[d][rows, :]
            h = av * h + u_refs[d][rows, :]
            p = av * p
            hl_refs[d][rows, :] = h
            p_refs[d][rows, :] = p
            new.append((h, p))
        return tuple(new)

    zeros = jnp.zeros((SCAN_SEGMENTS, LANES), F32)
    ones = jnp.ones((SCAN_SEGMENTS, LANES), F32)
    ends = lax.fori_loop(0, seg, step, ((zeros, ones), (zeros, ones)), unroll=SCAN_UNROLL)

    c_in = []
    for d in range(2):
        h_end, p_end = ends[d]
        order = range(SCAN_SEGMENTS) if d == 0 else range(SCAN_SEGMENTS - 1, -1, -1)
        c = jnp.zeros((1, LANES), F32)
        cs = [None] * SCAN_SEGMENTS
        for s in order:
            cs[s] = c
            c = p_end[s:s + 1, :] * c + h_end[s:s + 1, :]
        c_in.append(jnp.concatenate(cs, axis=0))

    def fix(j, _):
        rows = pl.ds(j, SCAN_SEGMENTS, stride=seg)
        acc_ref[rows, :] = ((hl_refs[0][rows, :] + p_refs[0][rows, :] * c_in[0])
                            + (hl_refs[1][rows, :] + p_refs[1][rows, :] * c_in[1]))
        return 0

    lax.fori_loop(0, seg, fix, 0, unroll=SCAN_UNROLL)
    o_ref[...] = acc_ref[N_META:, :]


def _rnn(xr, xr_meta, conv_w, conv_b, w4, b4, lam, *, batch, seq):
    n_slab, m, _ = xr.shape
    t_all = N_META + seq
    kern = functools.partial(_rnn_kernel, seq=seq)
    return pl.pallas_call(
        kern,
        grid=(batch, n_slab),
        in_specs=[
            pl.BlockSpec((1, seq, LANES), lambda b, n: (n, b, 0)),
            pl.BlockSpec((1, N_META, LANES), lambda b, n: (n, 0, 0)),
            pl.BlockSpec((CONV_W, LANES), lambda b, n: (0, n)),
            pl.BlockSpec((1, LANES), lambda b, n: (0, n)),
            pl.BlockSpec((1, LANES, 4 * LANES), lambda b, n: (n, 0, 0)),
            pl.BlockSpec((1, 4 * LANES), lambda b, n: (0, n)),
            pl.BlockSpec((2, LANES), lambda b, n: (0, n)),
        ],
        out_specs=pl.BlockSpec((seq, LANES), lambda b, n: (b, n)),
        out_shape=jax.ShapeDtypeStruct((m, n_slab * LANES), F32),
        scratch_shapes=[
            pltpu.VMEM((t_all + 2 * SUBLANES, LANES), F32),
        ] + [pltpu.VMEM((t_all, LANES), F32)] * 9,
        compiler_params=pltpu.CompilerParams(
            dimension_semantics=("arbitrary", "arbitrary"), vmem_limit_bytes=VMEM_LIMIT_BYTES),
        name="rnn",
    )(xr, xr_meta, conv_w, conv_b, w4, b4, lam)


def _post_kernel(x_ref, attn_ref, rnn_ref, gr_ref, gm_ref, wo_ref, g2_ref, wi_ref, wf_ref, o_ref,
                 *, d_model, d_ff):
    gates = jax.nn.sigmoid(gm_ref[...])
    rnn = rnn_ref[...] * jax.nn.gelu(gr_ref[...])
    mix = gates[:, :d_model] * attn_ref[...].astype(F32) + gates[:, d_model:] * rnn
    h1 = x_ref[...] + jnp.dot(mix.astype(BF16), wo_ref[...], preferred_element_type=F32)
    ms = jnp.mean(h1 * h1, axis=-1, keepdims=True)
    hn = (h1 * lax.rsqrt(ms + EPS) * g2_ref[...]).astype(BF16)
    gu = jnp.dot(hn, wi_ref[...], preferred_element_type=F32)
    act = (jax.nn.silu(gu[:, :d_ff]) * gu[:, d_ff:]).astype(BF16)
    o_ref[...] = h1 + jnp.dot(act, wf_ref[...], preferred_element_type=F32)


def _post(x2d, attn, rnn, gr, gm, w_out, g2, w_ffn_in, w_ffn_out, *, tm):
    m, d = x2d.shape
    d_ff = w_ffn_out.shape[0]
    kern = functools.partial(_post_kernel, d_model=d, d_ff=d_ff)
    row = lambda w: pl.BlockSpec((tm, w), lambda i: (i, 0))
    return pl.pallas_call(
        kern,
        grid=(m // tm,),
        in_specs=[row(d), row(d), row(d), row(d), row(2 * d),
                  _resident((d, d)), _resident((1, d)), _resident((d, 2 * d_ff)), _resident((d_ff, d))],
        out_specs=row(d),
        out_shape=jax.ShapeDtypeStruct((m, d), F32),
        compiler_params=pltpu.CompilerParams(
            dimension_semantics=("arbitrary",), vmem_limit_bytes=VMEM_LIMIT_BYTES),
        name="post",
    )(x2d, attn, rnn, gr, gm, w_out, g2, w_ffn_in, w_ffn_out)


def _rope_tables(seq):
    pos = jnp.arange(seq, dtype=jnp.int32)
    row = (pos // GRID_W).astype(F32)
    col = (pos % GRID_W).astype(F32)
    inv_freq = jnp.exp(-jnp.log(jnp.float32(ROPE_THETA)) * jnp.arange(ROPE_PAIRS, dtype=F32) / ROPE_PAIRS)
    ang_r = row[:, None] * inv_freq[None, :]
    ang_c = col[:, None] * inv_freq[None, :]
    cos = jnp.concatenate([jnp.cos(ang_r)] * 2 + [jnp.cos(ang_c)] * 2, axis=-1)
    sin = jnp.concatenate([-jnp.sin(ang_r), jnp.sin(ang_r), -jnp.sin(ang_c), jnp.sin(ang_c)], axis=-1)
    return cos, sin


def kernel(x, meta_tokens, norm1_g, w_in, conv_w, conv_b, rg_wa, rg_ba, rg_wx, rg_bx, rg_lambda,
           q_norm_g, k_norm_g, w_out, norm2_g, w_ffn_in, w_ffn_out):
    batch, seq, d = x.shape
    depth = norm1_g.shape[0]
    assert depth == 1, "meta-token rows are only skipped because no later layer reads them"
    q_width = N_Q_HEADS * HEAD_DIM
    kv_width = N_KV_HEADS * HEAD_DIM
    rnn_width = conv_w.shape[-1]
    n_slab = rnn_width // RNN_BLOCK
    assert meta_tokens.shape == (N_META, d) and seq % GRID_W == 0

    x2d = x.reshape(batch * seq, d)
    g1 = norm1_g[0].reshape(1, d)
    w_in_b = w_in[0].astype(BF16)
    qg = q_norm_g[0].reshape(1, HEAD_DIM)
    kg = k_norm_g[0].reshape(1, HEAD_DIM)
    cos, sin = _rope_tables(seq)
    cos_meta = jnp.ones((N_META, HEAD_DIM), F32)
    sin_meta = jnp.zeros((N_META, HEAD_DIM), F32)

    proj = functools.partial(_proj, q_width=q_width, kv_width=kv_width, rnn_width=rnn_width)
    q, k, vt, xr, gr, gm = proj(x2d, g1, w_in_b, qg, kg, cos, sin, tm=256)
    _, k_meta, vt_meta, xr_meta, _, _ = proj(
        meta_tokens.astype(x.dtype), g1, w_in_b, qg, kg, cos_meta, sin_meta, tm=N_META)

    attn = _attn(q, k, vt, k_meta, vt_meta, batch=batch, seq=seq, bq=256)

    w4 = 0.5 * jnp.concatenate([rg_wa[0, 0], rg_wx[0, 0], rg_wa[0, 1], rg_wx[0, 1]], axis=-1)
    w4 = w4.astype(BF16)
    b4 = 0.5 * jnp.stack([rg_ba[0, 0], rg_bx[0, 0], rg_ba[0, 1], rg_bx[0, 1]], axis=0)
    b4 = b4.reshape(4, n_slab, RNN_BLOCK).transpose(1, 0, 2).reshape(1, n_slab * 4 * RNN_BLOCK)
    rnn = _rnn(xr, xr_meta, conv_w[0], conv_b[0].reshape(1, rnn_width), w4, b4, rg_lambda[0],
               batch=batch, seq=seq)

    out = _post(x2d, attn, rnn, gr, gm, w_out[0].astype(BF16), norm2_g[0].reshape(1, d),
                w_ffn_in[0].astype(BF16), w_ffn_out[0].astype(BF16), tm=256)
    return out.reshape(batch, seq, d)
```

```python
import functools

import jax
import jax.numpy as jnp
from jax import lax
from jax.experimental import pallas as pl
from jax.experimental.pallas import tpu as pltpu

N_META = 16
GRID_W = 64
HEAD_DIM = 128
N_Q_HEADS = 8
N_KV_HEADS = 2
Q_GROUP = N_Q_HEADS // N_KV_HEADS
RNN_BLOCK = 128
CONV_W = 4
CONV_PAD_L = CONV_W // 2
RG_C = 8.0
ROPE_THETA = 10000.0
ROPE_PAIRS = HEAD_DIM // 4
EPS = 1e-6
LOG2_E = 1.4426950408889634
F32_TINY = 1.1754943508222875e-38

LANES = 128
SUBLANES = 8
VMEM_LIMIT_BYTES = 56 * 1024 * 1024

BF16 = jnp.bfloat16
F32 = jnp.float32


def _sigmoid(x):
    return 0.5 * jnp.tanh(0.5 * x) + 0.5


def _resident(shape):
    nd = len(shape)
    return pl.BlockSpec(shape, lambda *_: (0,) * nd, pipeline_mode=pl.Buffered(1))


def _swap_halves(x):
    lane = lax.broadcasted_iota(jnp.int32, x.shape, 1)
    lo = (lane % (2 * ROPE_PAIRS)) < ROPE_PAIRS
    return jnp.where(lo, pltpu.roll(x, LANES - ROPE_PAIRS, 1), pltpu.roll(x, ROPE_PAIRS, 1))


def _head_norm_rope(xh, g, cos, sin):
    ms = jnp.mean(xh * xh, axis=-1, keepdims=True)
    xn = xh * lax.rsqrt(ms + EPS) * g
    return xn * cos + _swap_halves(xn) * sin


def _proj_kernel(x_ref, g1_ref, w_ref, qg_ref, kg_ref, cos_ref, sin_ref,
                 q_ref, k_ref, vt_ref, xr_ref, gr_ref, gm_ref,
                 *, d_model, q_width, kv_width, rnn_width):
    x = x_ref[...]
    ms = jnp.mean(x * x, axis=-1, keepdims=True)
    xn = (x * lax.rsqrt(ms + EPS) * g1_ref[...]).astype(BF16)

    cos = cos_ref[...]
    sin = sin_ref[...]
    scale = HEAD_DIM ** -0.5 * LOG2_E

    o_k = q_width
    o_v = o_k + kv_width
    o_xr = o_v + kv_width
    o_gr = o_xr + rnn_width
    o_gm = o_gr + rnn_width

    project = lambda c0, c1: jnp.dot(xn, w_ref[:, c0:c1], preferred_element_type=F32)

    pq = project(0, o_k)
    for h in range(q_width // HEAD_DIM):
        sl = slice(h * HEAD_DIM, (h + 1) * HEAD_DIM)
        qh = _head_norm_rope(pq[:, sl], qg_ref[...], cos, sin)
        q_ref[:, sl] = (qh * scale).astype(BF16)

    pkv = project(o_k, o_xr)
    for h in range(kv_width // HEAD_DIM):
        sl = slice(h * HEAD_DIM, (h + 1) * HEAD_DIM)
        kh = _head_norm_rope(pkv[:, sl], kg_ref[...], cos, sin)
        k_ref[:, sl] = kh.astype(BF16)
    vt_ref[...] = pkv[:, kv_width:].T.astype(BF16)

    pxr = project(o_xr, o_gr)
    for n in range(rnn_width // RNN_BLOCK):
        xr_ref[n] = pxr[:, n * RNN_BLOCK:(n + 1) * RNN_BLOCK]

    gr_ref[...] = project(o_gr, o_gm)
    gm_ref[...] = project(o_gm, w_ref.shape[1])


def _proj(x2d, g1, w_in, qg, kg, cos, sin, *, tm, q_width, kv_width, rnn_width):
    m, d = x2d.shape
    in_width = w_in.shape[1]
    gm_width = in_width - q_width - 2 * kv_width - 2 * rnn_width
    n_tab = cos.shape[0] // tm
    n_slab = rnn_width // RNN_BLOCK
    kern = functools.partial(_proj_kernel, d_model=d, q_width=q_width, kv_width=kv_width,
                             rnn_width=rnn_width)
    return pl.pallas_call(
        kern,
        grid=(m // tm,),
        in_specs=[
            pl.BlockSpec((tm, d), lambda i: (i, 0)),
            _resident((1, d)),
            _resident((d, in_width)),
            _resident((1, HEAD_DIM)),
            _resident((1, HEAD_DIM)),
            pl.BlockSpec((tm, HEAD_DIM), lambda i: (i % n_tab, 0)),
            pl.BlockSpec((tm, HEAD_DIM), lambda i: (i % n_tab, 0)),
        ],
        out_specs=[
            pl.BlockSpec((tm, q_width), lambda i: (i, 0)),
            pl.BlockSpec((tm, kv_width), lambda i: (i, 0)),
            pl.BlockSpec((kv_width, tm), lambda i: (0, i)),
            pl.BlockSpec((n_slab, tm, RNN_BLOCK), lambda i: (0, i, 0)),
            pl.BlockSpec((tm, rnn_width), lambda i: (i, 0)),
            pl.BlockSpec((tm, gm_width), lambda i: (i, 0)),
        ],
        out_shape=[
            jax.ShapeDtypeStruct((m, q_width), BF16),
            jax.ShapeDtypeStruct((m, kv_width), BF16),
            jax.ShapeDtypeStruct((kv_width, m), BF16),
            jax.ShapeDtypeStruct((n_slab, m, RNN_BLOCK), F32),
            jax.ShapeDtypeStruct((m, rnn_width), F32),
            jax.ShapeDtypeStruct((m, gm_width), F32),
        ],
        compiler_params=pltpu.CompilerParams(
            dimension_semantics=("arbitrary",), vmem_limit_bytes=VMEM_LIMIT_BYTES),
        name="proj",
    )(x2d, g1, w_in, qg, kg, cos, sin)


ATTN_KEY_CHUNK = 512
ATTN_MAX_RAW_SCORE = 50.0


def _attn_kernel(raw_ok_ref, q_ref, k_ref, vt_ref, km_ref, vtm_ref, o_ref, *, bq):
    nt = (((1,), (1,)), ((), ()))
    seq = k_ref.shape[0]
    n_chunks = seq // ATTN_KEY_CHUNK
    chunk = lambda c: slice(c * ATTN_KEY_CHUNK, (c + 1) * ATTN_KEY_CHUNK)
    q = q_ref[...]
    qs = jnp.concatenate([q[:, g * HEAD_DIM:(g + 1) * HEAD_DIM] for g in range(Q_GROUP)], axis=0)

    def scores(c):
        return lax.dot_general(k_ref[chunk(c), :], qs, nt, preferred_element_type=F32)

    def finish(ot, l):
        ot = ot * (1.0 / l)
        for g in range(Q_GROUP):
            o_ref[:, g * HEAD_DIM:(g + 1) * HEAD_DIM] = ot[:, g * bq:(g + 1) * bq].T.astype(BF16)

    @pl.when(raw_ok_ref[0] == 1)
    def _raw_scores():
        pm = jnp.exp2(lax.dot_general(km_ref[...], qs, nt, preferred_element_type=F32))
        l = jnp.sum(pm, axis=0, keepdims=True)
        ot = jnp.dot(vtm_ref[...], pm.astype(BF16), preferred_element_type=F32)
        s_next = scores(0)
        for c in range(n_chunks):
            s = s_next
            if c + 1 < n_chunks:
                s_next = scores(c + 1)
            p = jnp.exp2(s)
            l = l + jnp.sum(p, axis=0, keepdims=True)
            ot = ot + jnp.dot(vt_ref[:, chunk(c)], p.astype(BF16), preferred_element_type=F32)
        finish(ot, l)

    @pl.when(raw_ok_ref[0] != 1)
    def _running_maximum():
        stm = lax.dot_general(km_ref[...], qs, nt, preferred_element_type=F32)
        m = jnp.max(stm, axis=0, keepdims=True)
        pm = jnp.exp2(stm - m)
        l = jnp.sum(pm, axis=0, keepdims=True)
        ot = jnp.dot(vtm_ref[...], pm.astype(BF16), preferred_element_type=F32)
        st_next = scores(0)
        for c in range(n_chunks):
            st = st_next
            if c + 1 < n_chunks:
                st_next = scores(c + 1)
            m_new = jnp.maximum(m, jnp.max(st, axis=0, keepdims=True))
            alpha = jnp.exp2(m - m_new)
            p = jnp.exp2(st - m_new)
            l = alpha * l + jnp.sum(p, axis=0, keepdims=True)
            ot = alpha * ot + jnp.dot(vt_ref[:, chunk(c)], p.astype(BF16), preferred_element_type=F32)
            m = m_new
        finish(ot, l)


def _attn(raw_ok, q, k, vt, k_meta, vt_meta, *, batch, seq, bq):
    m = q.shape[0]
    nq = seq // bq
    gw = Q_GROUP * HEAD_DIM
    kern = functools.partial(_attn_kernel, bq=bq)
    return pl.pallas_call(
        kern,
        grid=(batch, N_KV_HEADS, nq),
        in_specs=[
            pl.BlockSpec(memory_space=pltpu.SMEM),
            pl.BlockSpec((bq, gw), lambda b, h, i: (b * nq + i, h)),
            pl.BlockSpec((seq, HEAD_DIM), lambda b, h, i: (b, h)),
            pl.BlockSpec((HEAD_DIM, seq), lambda b, h, i: (h, b)),
            pl.BlockSpec((N_META, HEAD_DIM), lambda b, h, i: (0, h)),
            pl.BlockSpec((HEAD_DIM, N_META), lambda b, h, i: (h, 0)),
        ],
        out_specs=pl.BlockSpec((bq, gw), lambda b, h, i: (b * nq + i, h)),
        out_shape=jax.ShapeDtypeStruct((m, N_Q_HEADS * HEAD_DIM), BF16),
        compiler_params=pltpu.CompilerParams(
            dimension_semantics=("arbitrary", "arbitrary", "arbitrary"),
            vmem_limit_bytes=VMEM_LIMIT_BYTES),
        name="attn",
    )(raw_ok, q, k, vt, k_meta, vt_meta)


SCAN_SEGMENTS = 24
SCAN_UNROLL = True
GATE_BIAS_ROWS = 2


def _rnn_kernel(xr_ref, xm_ref, cw_ref, cb_ref, w4_ref, lam_ref, o_ref,
                xs_ref, a0_ref, a1_ref, u0_ref, u1_ref, h0_ref, h1_ref, *, seq):
    a_refs, u_refs, h_refs = (a0_ref, a1_ref), (u0_ref, u1_ref), (h0_ref, h1_ref)
    t_all = N_META + seq
    seg = t_all // SCAN_SEGMENTS
    n_vreg = SCAN_SEGMENTS // SUBLANES
    pad = SUBLANES

    xs_ref[0:pad, :] = jnp.zeros((pad, LANES), F32)
    xs_ref[pad:pad + N_META, :] = xm_ref[0]
    xs_ref[pad + N_META:pad + t_all, :] = xr_ref[0]
    xs_ref[pad + t_all:pad + t_all + pad, :] = jnp.zeros((pad, LANES), F32)

    xc = cb_ref[...]
    for j in range(CONV_W):
        off = pad + j - CONV_PAD_L
        xc = xc + xs_ref[off:off + t_all, :] * cw_ref[j:j + 1, :]
    xh = 0.5 * xc

    lane = lax.broadcasted_iota(jnp.int32, (t_all, LANES), 1)
    ones_cols = jnp.where(lane < GATE_BIAS_ROWS, 1.0, 0.0).astype(BF16)
    zh = jnp.dot(jnp.concatenate([xc.astype(BF16), ones_cols], axis=1), w4_ref[0],
                 preferred_element_type=F32)

    for d in range(2):
        t_r = jnp.tanh(zh[:, (2 * d) * LANES:(2 * d + 1) * LANES])
        t_i = jnp.tanh(zh[:, (2 * d + 1) * LANES:(2 * d + 2) * LANES])
        c_half = (0.5 * RG_C) * jax.nn.log_sigmoid(lam_ref[d:d + 1, :])
        log_a = c_half * t_r + c_half
        a = jnp.exp(log_a)
        a_refs[d][...] = a
        y = jnp.tanh(-log_a) * (a * a + 1.0)
        root = y * lax.rsqrt(jnp.maximum(y, F32_TINY))
        u_refs[d][...] = root * (t_i * xh + xh)

    def rows(d, jj, v):
        j = jj if d == 0 else seg - 1 - jj
        return pl.ds(j + v * SUBLANES * seg, SUBLANES, stride=seg)

    chains = [(d, v) for d in range(2) for v in range(n_vreg)]

    def sweep1(jj, carry):
        new = []
        for (d, v), (h, p) in zip(chains, carry):
            av = a_refs[d][rows(d, jj, v), :]
            new.append((av * h + u_refs[d][rows(d, jj, v), :], av * p))
        return tuple(new)

    zeros = jnp.zeros((SUBLANES, LANES), F32)
    ones = jnp.ones((SUBLANES, LANES), F32)
    ends = lax.fori_loop(0, seg, sweep1, ((zeros, ones),) * len(chains), unroll=SCAN_UNROLL)

    start = {}
    for d in range(2):
        order = range(SCAN_SEGMENTS) if d == 0 else range(SCAN_SEGMENTS - 1, -1, -1)
        c = jnp.zeros((1, LANES), F32)
        cs = [None] * SCAN_SEGMENTS
        for s in order:
            cs[s] = c
            h_end, p_end = ends[chains.index((d, s // SUBLANES))]
            r = s % SUBLANES
            c = p_end[r:r + 1, :] * c + h_end[r:r + 1, :]
        for v in range(n_vreg):
            start[(d, v)] = jnp.concatenate(cs[v * SUBLANES:(v + 1) * SUBLANES], axis=0)

    def sweep2(jj, carry):
        new = []
        for (d, v), h in zip(chains, carry):
            h = a_refs[d][rows(d, jj, v), :] * h + u_refs[d][rows(d, jj, v), :]
            h_refs[d][rows(d, jj, v), :] = h
            new.append(h)
        return tuple(new)

    lax.fori_loop(0, seg, sweep2, tuple(start[c] for c in chains), unroll=SCAN_UNROLL)
    o_ref[...] = h0_ref[N_META:, :] + h1_ref[N_META:, :]


def _rnn(xr, xr_meta, conv_w, conv_b, w4, lam, *, batch, seq):
    n_slab, m, _ = xr.shape
    t_all = N_META + seq
    kern = functools.partial(_rnn_kernel, seq=seq)
    return pl.pallas_call(
        kern,
        grid=(batch, n_slab),
        in_specs=[
            pl.BlockSpec((1, seq, LANES), lambda b, n: (n, b, 0)),
            pl.BlockSpec((1, N_META, LANES), lambda b, n: (n, 0, 0)),
            pl.BlockSpec((CONV_W, LANES), lambda b, n: (0, n)),
            pl.BlockSpec((1, LANES), lambda b, n: (0, n)),
            pl.BlockSpec((1, 2 * LANES, 4 * LANES), lambda b, n: (n, 0, 0)),
            pl.BlockSpec((2, LANES), lambda b, n: (0, n)),
        ],
        out_specs=pl.BlockSpec((seq, LANES), lambda b, n: (b, n)),
        out_shape=jax.ShapeDtypeStruct((m, n_slab * LANES), F32),
        scratch_shapes=[
            pltpu.VMEM((t_all + 2 * SUBLANES, LANES), F32),
        ] + [pltpu.VMEM((t_all, LANES), F32)] * 6,
        compiler_params=pltpu.CompilerParams(
            dimension_semantics=("arbitrary", "arbitrary"), vmem_limit_bytes=VMEM_LIMIT_BYTES),
        name="rnn",
    )(xr, xr_meta, conv_w, conv_b, w4, lam)


MXU_TILE = 256
POST_MIX_CHUNK = 256
POST_FFN_CHUNK = 1024


def _ffn_chunks(d_ff):
    first = d_ff % POST_FFN_CHUNK or POST_FFN_CHUNK
    assert first % MXU_TILE == 0
    bounds = [0, first] + list(range(first + POST_FFN_CHUNK, d_ff + 1, POST_FFN_CHUNK))
    return list(zip(bounds[:-1], bounds[1:]))


def _post_kernel(x_ref, attn_ref, rnn_ref, gr_ref, gm_ref, wo_ref, g2_ref, wi_ref, wf_ref, o_ref,
                 *, d_model, d_ff):
    gates = _sigmoid(gm_ref[...])
    rnn = rnn_ref[...] * jax.nn.gelu(gr_ref[...])
    mix = gates[:, :d_model] * attn_ref[...].astype(F32) + gates[:, d_model:] * rnn
    h1 = x_ref[...] + jnp.dot(mix.astype(BF16), wo_ref[...], preferred_element_type=F32)
    ms = jnp.mean(h1 * h1, axis=-1, keepdims=True)
    hn = (h1 * lax.rsqrt(ms + EPS) * g2_ref[...]).astype(BF16)
    out = h1
    for c0, c1 in _ffn_chunks(d_ff):
        g = jnp.dot(hn, wi_ref[:, c0:c1], preferred_element_type=F32)
        u = jnp.dot(hn, wi_ref[:, d_ff + c0:d_ff + c1], preferred_element_type=F32)
        act = (g * _sigmoid(g) * u).astype(BF16)
        out = out + jnp.dot(act, wf_ref[c0:c1, :], preferred_element_type=F32)
    o_ref[...] = out


def _post(x2d, attn, rnn, gr, gm, w_out, g2, w_ffn_in, w_ffn_out, *, tm):
    m, d = x2d.shape
    d_ff = w_ffn_out.shape[0]
    kern = functools.partial(_post_kernel, d_model=d, d_ff=d_ff)
    row = lambda w: pl.BlockSpec((tm, w), lambda i: (i, 0))
    return pl.pallas_call(
        kern,
        grid=(m // tm,),
        in_specs=[row(d), row(d), row(d), row(d), row(2 * d),
                  _resident((d, d)), _resident((1, d)), _resident((d, 2 * d_ff)), _resident((d_ff, d))],
        out_specs=row(d),
        out_shape=jax.ShapeDtypeStruct((m, d), F32),
        compiler_params=pltpu.CompilerParams(
            dimension_semantics=("arbitrary",), vmem_limit_bytes=VMEM_LIMIT_BYTES),
        name="post",
    )(x2d, attn, rnn, gr, gm, w_out, g2, w_ffn_in, w_ffn_out)


def _rope_tables(seq):
    pos = jnp.arange(seq, dtype=jnp.int32)
    row = (pos // GRID_W).astype(F32)
    col = (pos % GRID_W).astype(F32)
    inv_freq = jnp.exp(-jnp.log(jnp.float32(ROPE_THETA)) * jnp.arange(ROPE_PAIRS, dtype=F32) / ROPE_PAIRS)
    ang_r = row[:, None] * inv_freq[None, :]
    ang_c = col[:, None] * inv_freq[None, :]
    cos = jnp.concatenate([jnp.cos(ang_r)] * 2 + [jnp.cos(ang_c)] * 2, axis=-1)
    sin = jnp.concatenate([-jnp.sin(ang_r), jnp.sin(ang_r), -jnp.sin(ang_c), jnp.sin(ang_c)], axis=-1)
    return cos, sin


def kernel(x, meta_tokens, norm1_g, w_in, conv_w, conv_b, rg_wa, rg_ba, rg_wx, rg_bx, rg_lambda,
           q_norm_g, k_norm_g, w_out, norm2_g, w_ffn_in, w_ffn_out):
    batch, seq, d = x.shape
    depth = norm1_g.shape[0]
    assert depth == 1, "meta-token rows are only skipped because no later layer reads them"
    q_width = N_Q_HEADS * HEAD_DIM
    kv_width = N_KV_HEADS * HEAD_DIM
    rnn_width = conv_w.shape[-1]
    n_slab = rnn_width // RNN_BLOCK
    assert meta_tokens.shape == (N_META, d) and seq % GRID_W == 0

    x2d = x.reshape(batch * seq, d)
    g1 = norm1_g[0].reshape(1, d)
    w_in_b = w_in[0].astype(BF16)
    qg = q_norm_g[0].reshape(1, HEAD_DIM)
    kg = k_norm_g[0].reshape(1, HEAD_DIM)
    cos, sin = _rope_tables(seq)
    cos_meta = jnp.ones((N_META, HEAD_DIM), F32)
    sin_meta = jnp.zeros((N_META, HEAD_DIM), F32)

    proj = functools.partial(_proj, q_width=q_width, kv_width=kv_width, rnn_width=rnn_width)
    q, k, vt, xr, gr, gm = proj(x2d, g1, w_in_b, qg, kg, cos, sin, tm=256)
    _, k_meta, vt_meta, xr_meta, _, _ = proj(
        meta_tokens.astype(x.dtype), g1, w_in_b, qg, kg, cos_meta, sin_meta, tm=N_META)

    score_bound = (HEAD_DIM * HEAD_DIM ** -0.5 * LOG2_E) * jnp.max(jnp.abs(qg)) * jnp.max(jnp.abs(kg))
    raw_ok = (score_bound <= ATTN_MAX_RAW_SCORE).astype(jnp.int32).reshape(1)
    attn = _attn(raw_ok, q, k, vt, k_meta, vt_meta, batch=batch, seq=seq, bq=512)

    w4 = 0.5 * jnp.concatenate([rg_wa[0, 0], rg_wx[0, 0], rg_wa[0, 1], rg_wx[0, 1]], axis=-1)
    b4 = 0.5 * jnp.stack([rg_ba[0, 0], rg_bx[0, 0], rg_ba[0, 1], rg_bx[0, 1]], axis=0)
    b4 = b4.reshape(4, n_slab, RNN_BLOCK).transpose(1, 0, 2).reshape(n_slab, 1, 4 * RNN_BLOCK)
    b4_hi = b4.astype(BF16)
    b4_lo = (b4 - b4_hi.astype(F32)).astype(BF16)
    w4_pad = jnp.zeros((n_slab, RNN_BLOCK - GATE_BIAS_ROWS, 4 * RNN_BLOCK), BF16)
    w4 = jnp.concatenate([w4.astype(BF16), b4_hi, b4_lo, w4_pad], axis=1)
    rnn = _rnn(xr, xr_meta, conv_w[0], conv_b[0].reshape(1, rnn_width), w4, rg_lambda[0],
               batch=batch, seq=seq)

    out = _post(x2d, attn, rnn, gr, gm, w_out[0].astype(BF16), norm2_g[0].reshape(1, d),
                w_ffn_in[0].astype(BF16), w_ffn_out[0].astype(BF16), tm=256)
    return out.reshape(batch, seq, d)
```

```python
import functools

import jax
import jax.numpy as jnp
from jax import lax
from jax.experimental import pallas as pl
from jax.experimental.pallas import tpu as pltpu

N_META = 16
GRID_W = 64
HEAD_DIM = 128
N_Q_HEADS = 8
N_KV_HEADS = 2
Q_GROUP = N_Q_HEADS // N_KV_HEADS
RNN_BLOCK = 128
CONV_W = 4
CONV_PAD_L = CONV_W // 2
RG_C = 8.0
ROPE_THETA = 10000.0
ROPE_PAIRS = HEAD_DIM // 4
EPS = 1e-6
LOG2_E = 1.4426950408889634
F32_TINY = 1.1754943508222875e-38

LANES = 128
SUBLANES = 8
VMEM_LIMIT_BYTES = 56 * 1024 * 1024

BF16 = jnp.bfloat16
F32 = jnp.float32


def _sigmoid(x):
    return 0.5 * jnp.tanh(0.5 * x) + 0.5


def _resident(shape):
    nd = len(shape)
    return pl.BlockSpec(shape, lambda *_: (0,) * nd, pipeline_mode=pl.Buffered(1))


def _swap_halves(x):
    lane = lax.broadcasted_iota(jnp.int32, x.shape, 1)
    lo = (lane % (2 * ROPE_PAIRS)) < ROPE_PAIRS
    return jnp.where(lo, pltpu.roll(x, LANES - ROPE_PAIRS, 1), pltpu.roll(x, ROPE_PAIRS, 1))


def _head_norm_rope(xh, g, cos, sin):
    ms = jnp.mean(xh * xh, axis=-1, keepdims=True)
    xn = xh * lax.rsqrt(ms + EPS) * g
    return xn * cos + _swap_halves(xn) * sin


def _proj_kernel(x_ref, g1_ref, w_ref, qg_ref, kg_ref, cos_ref, sin_ref,
                 q_ref, k_ref, vt_ref, xr_ref, gr_ref, gm_ref,
                 *, d_model, q_width, kv_width, rnn_width):
    x = x_ref[...]
    ms = jnp.mean(x * x, axis=-1, keepdims=True)
    xn = (x * lax.rsqrt(ms + EPS) * g1_ref[...]).astype(BF16)

    cos = cos_ref[...]
    sin = sin_ref[...]
    scale = HEAD_DIM ** -0.5 * LOG2_E

    o_k = q_width
    o_v = o_k + kv_width
    o_xr = o_v + kv_width
    o_gr = o_xr + rnn_width
    o_gm = o_gr + rnn_width

    project = lambda c0, c1: jnp.dot(xn, w_ref[:, c0:c1], preferred_element_type=F32)

    pq = project(0, o_k)
    for h in range(q_width // HEAD_DIM):
        sl = slice(h * HEAD_DIM, (h + 1) * HEAD_DIM)
        qh = _head_norm_rope(pq[:, sl], qg_ref[...], cos, sin)
        q_ref[:, sl] = (qh * scale).astype(BF16)

    pkv = project(o_k, o_xr)
    for h in range(kv_width // HEAD_DIM):
        sl = slice(h * HEAD_DIM, (h + 1) * HEAD_DIM)
        kh = _head_norm_rope(pkv[:, sl], kg_ref[...], cos, sin)
        k_ref[:, sl] = kh.astype(BF16)
    vt_ref[...] = pkv[:, kv_width:].T.astype(BF16)

    pxr = project(o_xr, o_gr)
    for n in range(rnn_width // RNN_BLOCK):
        xr_ref[n] = pxr[:, n * RNN_BLOCK:(n + 1) * RNN_BLOCK]

    gr_ref[...] = project(o_gr, o_gm)
    gm_ref[...] = project(o_gm, w_ref.shape[1])


def _proj(x2d, g1, w_in, qg, kg, cos, sin, *, tm, q_width, kv_width, rnn_width):
    m, d = x2d.shape
    in_width = w_in.shape[1]
    gm_width = in_width - q_width - 2 * kv_width - 2 * rnn_width
    n_tab = cos.shape[0] // tm
    n_slab = rnn_width // RNN_BLOCK
    kern = functools.partial(_proj_kernel, d_model=d, q_width=q_width, kv_width=kv_width,
                             rnn_width=rnn_width)
    return pl.pallas_call(
        kern,
        grid=(m // tm,),
        in_specs=[
            pl.BlockSpec((tm, d), lambda i: (i, 0)),
            _resident((1, d)),
            _resident((d, in_width)),
            _resident((1, HEAD_DIM)),
            _resident((1, HEAD_DIM)),
            pl.BlockSpec((tm, HEAD_DIM), lambda i: (i % n_tab, 0)),
            pl.BlockSpec((tm, HEAD_DIM), lambda i: (i % n_tab, 0)),
        ],
        out_specs=[
            pl.BlockSpec((tm, q_width), lambda i: (i, 0)),
            pl.BlockSpec((tm, kv_width), lambda i: (i, 0)),
            pl.BlockSpec((kv_width, tm), lambda i: (0, i)),
            pl.BlockSpec((n_slab, tm, RNN_BLOCK), lambda i: (0, i, 0)),
            pl.BlockSpec((tm, rnn_width), lambda i: (i, 0)),
            pl.BlockSpec((tm, gm_width), lambda i: (i, 0)),
        ],
        out_shape=[
            jax.ShapeDtypeStruct((m, q_width), BF16),
            jax.ShapeDtypeStruct((m, kv_width), BF16),
            jax.ShapeDtypeStruct((kv_width, m), BF16),
            jax.ShapeDtypeStruct((n_slab, m, RNN_BLOCK), F32),
            jax.ShapeDtypeStruct((m, rnn_width), F32),
            jax.ShapeDtypeStruct((m, gm_width), F32),
        ],
        compiler_params=pltpu.CompilerParams(
            dimension_semantics=("arbitrary",), vmem_limit_bytes=VMEM_LIMIT_BYTES),
        name="proj",
    )(x2d, g1, w_in, qg, kg, cos, sin)


ATTN_KEY_CHUNK = 1024
ATTN_MAX_RAW_SCORE = 50.0

SCAN_SEGMENTS = 48
GATE_BIAS_ROWS = 2


def _rnn_stages(xr_ref, xm_ref, cw_ref, cb_ref, w4_ref, lam_ref, o_ref,
                xs_ref, a_refs, u_refs, h_refs, *, seq):
    t_all = N_META + seq
    seg = t_all // SCAN_SEGMENTS
    n_vreg = SCAN_SEGMENTS // SUBLANES
    pad = SUBLANES
    chains = [(d, v) for d in range(2) for v in range(n_vreg)]
    live = {}

    def rows(d, jj, v):
        j = jj if d == 0 else seg - 1 - jj
        return pl.ds(j + v * SUBLANES * seg, SUBLANES, stride=seg)

    def conv():
        xs_ref[0:pad, :] = jnp.zeros((pad, LANES), F32)
        xs_ref[pad:pad + N_META, :] = xm_ref[0]
        xs_ref[pad + N_META:pad + t_all, :] = xr_ref[0]
        xs_ref[pad + t_all:pad + t_all + pad, :] = jnp.zeros((pad, LANES), F32)
        xc = cb_ref[...]
        for j in range(CONV_W):
            off = pad + j - CONV_PAD_L
            xc = xc + xs_ref[off:off + t_all, :] * cw_ref[j:j + 1, :]
        live["xh"] = 0.5 * xc
        live["xc16"] = xc.astype(BF16)

    def gate_matmul():
        lane = lax.broadcasted_iota(jnp.int32, (t_all, LANES), 1)
        ones_cols = jnp.where(lane < GATE_BIAS_ROWS, 1.0, 0.0).astype(BF16)
        live["zh"] = jnp.dot(jnp.concatenate([live["xc16"], ones_cols], axis=1), w4_ref[0],
                             preferred_element_type=F32)

    def decay_and_input(d):
        zh, xh = live["zh"], live["xh"]
        t_r = jnp.tanh(zh[:, (2 * d) * LANES:(2 * d + 1) * LANES])
        t_i = jnp.tanh(zh[:, (2 * d + 1) * LANES:(2 * d + 2) * LANES])
        c_half = (0.5 * RG_C) * jax.nn.log_sigmoid(lam_ref[d:d + 1, :])
        log_a = c_half * t_r + c_half
        a = jnp.exp(log_a)
        a_refs[d][...] = a
        y = jnp.tanh(-log_a) * (a * a + 1.0)
        root = y * lax.rsqrt(jnp.maximum(y, F32_TINY))
        u_refs[d][...] = root * (t_i * xh + xh)

    def segment_end_states():
        carry = [(jnp.zeros((SUBLANES, LANES), F32), jnp.ones((SUBLANES, LANES), F32))] * len(chains)
        for jj in range(seg):
            new = []
            for (d, v), (h, p) in zip(chains, carry):
                av = a_refs[d][rows(d, jj, v), :]
                new.append((av * h + u_refs[d][rows(d, jj, v), :], av * p))
            carry = new
        start = {}
        for d in range(2):
            order = range(SCAN_SEGMENTS) if d == 0 else range(SCAN_SEGMENTS - 1, -1, -1)
            c = jnp.zeros((1, LANES), F32)
            cs = [None] * SCAN_SEGMENTS
            for s in order:
                cs[s] = c
                h_end, p_end = carry[chains.index((d, s // SUBLANES))]
                r = s % SUBLANES
                c = p_end[r:r + 1, :] * c + h_end[r:r + 1, :]
            for v in range(n_vreg):
                start[(d, v)] = jnp.concatenate(cs[v * SUBLANES:(v + 1) * SUBLANES], axis=0)
        live["start"] = start

    def states_and_output():
        carry = [live["start"][c] for c in chains]
        for jj in range(seg):
            new = []
            for (d, v), h in zip(chains, carry):
                h = a_refs[d][rows(d, jj, v), :] * h + u_refs[d][rows(d, jj, v), :]
                h_refs[d][rows(d, jj, v), :] = h
                new.append(h)
            carry = new
        o_ref[...] = h_refs[0][N_META:, :] + h_refs[1][N_META:, :]

    return [conv, gate_matmul, functools.partial(decay_and_input, 0),
            functools.partial(decay_and_input, 1), segment_end_states, states_and_output]


def _rnn_kernel(xr_ref, xm_ref, cw_ref, cb_ref, w4_ref, lam_ref, o_ref,
                xs_ref, a0_ref, a1_ref, u0_ref, u1_ref, h0_ref, h1_ref, *, seq):
    for stage in _rnn_stages(xr_ref, xm_ref, cw_ref, cb_ref, w4_ref, lam_ref, o_ref, xs_ref,
                             (a0_ref, a1_ref), (u0_ref, u1_ref), (h0_ref, h1_ref), seq=seq):
        stage()


def _attn_kernel(raw_ok_ref, q_ref, k_ref, vt_ref, km_ref, vtm_ref, o_ref, *, bq):
    nt = (((1,), (1,)), ((), ()))
    seq = k_ref.shape[0]
    n_chunks = seq // ATTN_KEY_CHUNK
    chunk = lambda c: slice(c * ATTN_KEY_CHUNK, (c + 1) * ATTN_KEY_CHUNK)

    def attend(running_maximum):
        q = q_ref[...]
        qs = jnp.concatenate([q[:, g * HEAD_DIM:(g + 1) * HEAD_DIM] for g in range(Q_GROUP)], axis=0)
        scores = lambda c: lax.dot_general(k_ref[chunk(c), :], qs, nt, preferred_element_type=F32)

        sm = lax.dot_general(km_ref[...], qs, nt, preferred_element_type=F32)
        m = jnp.max(sm, axis=0, keepdims=True) if running_maximum else None
        pm = jnp.exp2(sm - m) if running_maximum else jnp.exp2(sm)
        l = jnp.sum(pm, axis=0, keepdims=True)
        ot = jnp.dot(vtm_ref[...], pm.astype(BF16), preferred_element_type=F32)
        s_next = scores(0)
        for c in range(n_chunks):
            s = s_next
            if c + 1 < n_chunks:
                s_next = scores(c + 1)
            if running_maximum:
                m_new = jnp.maximum(m, jnp.max(s, axis=0, keepdims=True))
                alpha = jnp.exp2(m - m_new)
                p = jnp.exp2(s - m_new)
                l = alpha * l + jnp.sum(p, axis=0, keepdims=True)
                ot = alpha * ot
                m = m_new
            else:
                p = jnp.exp2(s)
                l = l + jnp.sum(p, axis=0, keepdims=True)
            ot = ot + jnp.dot(vt_ref[:, chunk(c)], p.astype(BF16), preferred_element_type=F32)
        ot = ot * (1.0 / l)
        for g in range(Q_GROUP):
            o_ref[:, g * HEAD_DIM:(g + 1) * HEAD_DIM] = ot[:, g * bq:(g + 1) * bq].T.astype(BF16)

    @pl.when(raw_ok_ref[0] == 1)
    def _raw_scores():
        attend(running_maximum=False)

    @pl.when(raw_ok_ref[0] != 1)
    def _running_maximum():
        attend(running_maximum=True)


def _attn(raw_ok, q, k, vt, k_meta, vt_meta, *, batch, seq, bq):
    m = q.shape[0]
    nq = seq // bq
    gw = Q_GROUP * HEAD_DIM
    assert seq % ATTN_KEY_CHUNK == 0
    kern = functools.partial(_attn_kernel, bq=bq)
    return pl.pallas_call(
        kern,
        grid=(batch, N_KV_HEADS, nq),
        in_specs=[
            pl.BlockSpec(memory_space=pltpu.SMEM),
            pl.BlockSpec((bq, gw), lambda b, h, i: (b * nq + i, h)),
            pl.BlockSpec((seq, HEAD_DIM), lambda b, h, i: (b, h)),
            pl.BlockSpec((HEAD_DIM, seq), lambda b, h, i: (h, b)),
            pl.BlockSpec((N_META, HEAD_DIM), lambda b, h, i: (0, h)),
            pl.BlockSpec((HEAD_DIM, N_META), lambda b, h, i: (h, 0)),
        ],
        out_specs=pl.BlockSpec((bq, gw), lambda b, h, i: (b * nq + i, h)),
        out_shape=jax.ShapeDtypeStruct((m, N_Q_HEADS * HEAD_DIM), BF16),
        compiler_params=pltpu.CompilerParams(
            dimension_semantics=("arbitrary", "arbitrary", "arbitrary"),
            vmem_limit_bytes=VMEM_LIMIT_BYTES),
        name="attn",
    )(raw_ok, q, k, vt, k_meta, vt_meta)


def _rnn(xr, xr_meta, conv_w, conv_b, w4, lam, *, batch, seq):
    n_slab, m, _ = xr.shape
    t_all = N_META + seq
    assert t_all % SCAN_SEGMENTS == 0
    kern = functools.partial(_rnn_kernel, seq=seq)
    return pl.pallas_call(
        kern,
        grid=(batch, n_slab),
        in_specs=[
            pl.BlockSpec((1, seq, LANES), lambda b, n: (n, b, 0)),
            pl.BlockSpec((1, N_META, LANES), lambda b, n: (n, 0, 0)),
            pl.BlockSpec((CONV_W, LANES), lambda b, n: (0, n)),
            pl.BlockSpec((1, LANES), lambda b, n: (0, n)),
            pl.BlockSpec((1, 2 * LANES, 4 * LANES), lambda b, n: (n, 0, 0)),
            pl.BlockSpec((2, LANES), lambda b, n: (0, n)),
        ],
        out_specs=pl.BlockSpec((seq, LANES), lambda b, n: (b, n)),
        out_shape=jax.ShapeDtypeStruct((m, n_slab * LANES), F32),
        scratch_shapes=[
            pltpu.VMEM((t_all + 2 * SUBLANES, LANES), F32),
        ] + [pltpu.VMEM((t_all, LANES), F32)] * 6,
        compiler_params=pltpu.CompilerParams(
            dimension_semantics=("arbitrary", "arbitrary"), vmem_limit_bytes=VMEM_LIMIT_BYTES),
        name="rnn",
    )(xr, xr_meta, conv_w, conv_b, w4, lam)


MXU_TILE = 256
POST_FFN_CHUNK = 512


def _ffn_chunks(d_ff):
    first = d_ff % POST_FFN_CHUNK or POST_FFN_CHUNK
    assert first % MXU_TILE == 0
    bounds = [0, first] + list(range(first + POST_FFN_CHUNK, d_ff + 1, POST_FFN_CHUNK))
    return list(zip(bounds[:-1], bounds[1:]))


def _post_kernel(x_ref, attn_ref, rnn_ref, gr_ref, gm_ref, wo_ref, g2_ref, wi_ref, wf_ref, o_ref,
                 *, d_model, d_ff):
    gates = _sigmoid(gm_ref[...])
    rnn = rnn_ref[...] * jax.nn.gelu(gr_ref[...])
    mix = gates[:, :d_model] * attn_ref[...].astype(F32) + gates[:, d_model:] * rnn
    h1 = x_ref[...] + jnp.dot(mix.astype(BF16), wo_ref[...], preferred_element_type=F32)
    ms = jnp.mean(h1 * h1, axis=-1, keepdims=True)
    hn = (h1 * lax.rsqrt(ms + EPS) * g2_ref[...]).astype(BF16)
    out = h1
    for c0, c1 in _ffn_chunks(d_ff):
        g = jnp.dot(hn, wi_ref[:, c0:c1], preferred_element_type=F32)
        u = jnp.dot(hn, wi_ref[:, d_ff + c0:d_ff + c1], preferred_element_type=F32)
        act = (g * _sigmoid(g) * u).astype(BF16)
        out = out + jnp.dot(act, wf_ref[c0:c1, :], preferred_element_type=F32)
    o_ref[...] = out


def _post(x2d, attn, rnn, gr, gm, w_out, g2, w_ffn_in, w_ffn_out, *, tm):
    m, d = x2d.shape
    d_ff = w_ffn_out.shape[0]
    kern = functools.partial(_post_kernel, d_model=d, d_ff=d_ff)
    row = lambda w: pl.BlockSpec((tm, w), lambda i: (i, 0))
    return pl.pallas_call(
        kern,
        grid=(m // tm,),
        in_specs=[row(d), row(d), row(d), row(d), row(2 * d),
                  _resident((d, d)), _resident((1, d)), _resident((d, 2 * d_ff)), _resident((d_ff, d))],
        out_specs=row(d),
        out_shape=jax.ShapeDtypeStruct((m, d), F32),
        compiler_params=pltpu.CompilerParams(
            dimension_semantics=("arbitrary",), vmem_limit_bytes=VMEM_LIMIT_BYTES),
        name="post",
    )(x2d, attn, rnn, gr, gm, w_out, g2, w_ffn_in, w_ffn_out)


ROW_TILE = 512
ATTN_QUERY_BLOCK = 512


def _rope_tables(seq):
    pos = jnp.arange(seq, dtype=jnp.int32)
    row = (pos // GRID_W).astype(F32)
    col = (pos % GRID_W).astype(F32)
    inv_freq = jnp.exp(-jnp.log(jnp.float32(ROPE_THETA)) * jnp.arange(ROPE_PAIRS, dtype=F32) / ROPE_PAIRS)
    ang_r = row[:, None] * inv_freq[None, :]
    ang_c = col[:, None] * inv_freq[None, :]
    cos = jnp.concatenate([jnp.cos(ang_r)] * 2 + [jnp.cos(ang_c)] * 2, axis=-1)
    sin = jnp.concatenate([-jnp.sin(ang_r), jnp.sin(ang_r), -jnp.sin(ang_c), jnp.sin(ang_c)], axis=-1)
    return cos, sin


def kernel(x, meta_tokens, norm1_g, w_in, conv_w, conv_b, rg_wa, rg_ba, rg_wx, rg_bx, rg_lambda,
           q_norm_g, k_norm_g, w_out, norm2_g, w_ffn_in, w_ffn_out):
    batch, seq, d = x.shape
    depth = norm1_g.shape[0]
    assert depth == 1, "meta-token rows are only skipped because no later layer reads them"
    q_width = N_Q_HEADS * HEAD_DIM
    kv_width = N_KV_HEADS * HEAD_DIM
    rnn_width = conv_w.shape[-1]
    n_slab = rnn_width // RNN_BLOCK
    assert meta_tokens.shape == (N_META, d) and seq % GRID_W == 0 and seq % ROW_TILE == 0

    x2d = x.reshape(batch * seq, d)
    g1 = norm1_g[0].reshape(1, d)
    w_in_b = w_in[0].astype(BF16)
    qg = q_norm_g[0].reshape(1, HEAD_DIM)
    kg = k_norm_g[0].reshape(1, HEAD_DIM)
    cos, sin = _rope_tables(seq)
    cos_meta = jnp.ones((N_META, HEAD_DIM), F32)
    sin_meta = jnp.zeros((N_META, HEAD_DIM), F32)

    proj = functools.partial(_proj, q_width=q_width, kv_width=kv_width, rnn_width=rnn_width)
    q, k, vt, xr, gr, gm = proj(x2d, g1, w_in_b, qg, kg, cos, sin, tm=ROW_TILE)
    _, k_meta, vt_meta, xr_meta, _, _ = proj(
        meta_tokens.astype(x.dtype), g1, w_in_b, qg, kg, cos_meta, sin_meta, tm=N_META)

    score_bound = (HEAD_DIM * HEAD_DIM ** -0.5 * LOG2_E) * jnp.max(jnp.abs(qg)) * jnp.max(jnp.abs(kg))
    raw_ok = (score_bound <= ATTN_MAX_RAW_SCORE).astype(jnp.int32).reshape(1)

    w4 = 0.5 * jnp.concatenate([rg_wa[0, 0], rg_wx[0, 0], rg_wa[0, 1], rg_wx[0, 1]], axis=-1)
    b4 = 0.5 * jnp.stack([rg_ba[0, 0], rg_bx[0, 0], rg_ba[0, 1], rg_bx[0, 1]], axis=0)
    b4 = b4.reshape(4, n_slab, RNN_BLOCK).transpose(1, 0, 2).reshape(n_slab, 1, 4 * RNN_BLOCK)
    b4_hi = b4.astype(BF16)
    b4_lo = (b4 - b4_hi.astype(F32)).astype(BF16)
    w4_pad = jnp.zeros((n_slab, RNN_BLOCK - GATE_BIAS_ROWS, 4 * RNN_BLOCK), BF16)
    w4 = jnp.concatenate([w4.astype(BF16), b4_hi, b4_lo, w4_pad], axis=1)
    attn = _attn(raw_ok, q, k, vt, k_meta, vt_meta, batch=batch, seq=seq, bq=ATTN_QUERY_BLOCK)
    rnn = _rnn(xr, xr_meta, conv_w[0], conv_b[0].reshape(1, rnn_width), w4, rg_lambda[0],
               batch=batch, seq=seq)

    out = _post(x2d, attn, rnn, gr, gm, w_out[0].astype(BF16), norm2_g[0].reshape(1, d),
                w_ffn_in[0].astype(BF16), w_ffn_out[0].astype(BF16), tm=ROW_TILE)
    return out.reshape(batch, seq, d)
```

```python
import functools

import jax
import jax.numpy as jnp
from jax import lax
from jax.experimental import pallas as pl
from jax.experimental.pallas import tpu as pltpu

N_META = 16
GRID_W = 64
HEAD_DIM = 128
N_Q_HEADS = 8
N_KV_HEADS = 2
Q_GROUP = N_Q_HEADS // N_KV_HEADS
RNN_BLOCK = 128
CONV_W = 4
CONV_PAD_L = CONV_W // 2
RG_C = 8.0
ROPE_THETA = 10000.0
ROPE_PAIRS = HEAD_DIM // 4
EPS = 1e-6
LOG2_E = 1.4426950408889634
F32_TINY = 1.1754943508222875e-38

GATE_INPUT_SCALE = 0.5
GELU_INPUT_SCALE = 0.25
ATTN_OUT_SCALE = 0.5
GELU_K1 = 0.7978845608028654 / GELU_INPUT_SCALE
GELU_K3 = 0.7978845608028654 * 0.044715 / GELU_INPUT_SCALE ** 3

LANES = 128
SUBLANES = 8
VMEM_LIMIT_BYTES = 56 * 1024 * 1024

BF16 = jnp.bfloat16
F32 = jnp.float32


def _sigmoid(x):
    return 0.5 * jnp.tanh(0.5 * x) + 0.5


def _resident(shape):
    nd = len(shape)
    return pl.BlockSpec(shape, lambda *_: (0,) * nd, pipeline_mode=pl.Buffered(1))


def _swap_halves(x):
    lane = lax.broadcasted_iota(jnp.int32, x.shape, 1)
    lo = (lane % (2 * ROPE_PAIRS)) < ROPE_PAIRS
    return jnp.where(lo, pltpu.roll(x, LANES - ROPE_PAIRS, 1), pltpu.roll(x, ROPE_PAIRS, 1))


def _head_norm_rope(xh, g, cos, sin):
    ms = jnp.mean(xh * xh, axis=-1, keepdims=True)
    xn = xh * lax.rsqrt(ms + EPS) * g
    return xn * cos + _swap_halves(xn) * sin


def _proj_kernel(x_ref, g1_ref, w_ref, qg_ref, kg_ref, cos_ref, sin_ref,
                 q_ref, k_ref, vt_ref, xr_ref, gr_ref, gm_ref,
                 *, d_model, q_width, kv_width, rnn_width):
    x = x_ref[...]
    ms = jnp.mean(x * x, axis=-1, keepdims=True)
    xn = (x * lax.rsqrt(ms + EPS) * g1_ref[...]).astype(BF16)

    cos = cos_ref[...]
    sin = sin_ref[...]
    scale = HEAD_DIM ** -0.5 * LOG2_E

    o_k = q_width
    o_v = o_k + kv_width
    o_xr = o_v + kv_width
    o_gr = o_xr + rnn_width
    o_gm = o_gr + rnn_width

    project = lambda c0, c1: jnp.dot(xn, w_ref[:, c0:c1], preferred_element_type=F32)

    pq = project(0, o_k)
    for h in range(q_width // HEAD_DIM):
        sl = slice(h * HEAD_DIM, (h + 1) * HEAD_DIM)
        qh = _head_norm_rope(pq[:, sl], qg_ref[...], cos, sin)
        q_ref[:, sl] = (qh * scale).astype(BF16)

    pkv = project(o_k, o_xr)
    for h in range(kv_width // HEAD_DIM):
        sl = slice(h * HEAD_DIM, (h + 1) * HEAD_DIM)
        kh = _head_norm_rope(pkv[:, sl], kg_ref[...], cos, sin)
        k_ref[:, sl] = kh.astype(BF16)
    vt_ref[...] = pkv[:, kv_width:].T.astype(BF16)

    pxr = project(o_xr, o_gr)
    for n in range(rnn_width // RNN_BLOCK):
        xr_ref[n] = pxr[:, n * RNN_BLOCK:(n + 1) * RNN_BLOCK]

    gr_ref[...] = project(o_gr, o_gm)
    gm_ref[...] = project(o_gm, w_ref.shape[1])


def _proj(x2d, g1, w_in, qg, kg, cos, sin, *, tm, q_width, kv_width, rnn_width):
    m, d = x2d.shape
    in_width = w_in.shape[1]
    gm_width = in_width - q_width - 2 * kv_width - 2 * rnn_width
    n_tab = cos.shape[0] // tm
    n_slab = rnn_width // RNN_BLOCK
    kern = functools.partial(_proj_kernel, d_model=d, q_width=q_width, kv_width=kv_width,
                             rnn_width=rnn_width)
    return pl.pallas_call(
        kern,
        grid=(m // tm,),
        in_specs=[
            pl.BlockSpec((tm, d), lambda i: (i, 0)),
            _resident((1, d)),
            _resident((d, in_width)),
            _resident((1, HEAD_DIM)),
            _resident((1, HEAD_DIM)),
            pl.BlockSpec((tm, HEAD_DIM), lambda i: (i % n_tab, 0)),
            pl.BlockSpec((tm, HEAD_DIM), lambda i: (i % n_tab, 0)),
        ],
        out_specs=[
            pl.BlockSpec((tm, q_width), lambda i: (i, 0)),
            pl.BlockSpec((tm, kv_width), lambda i: (i, 0)),
            pl.BlockSpec((kv_width, tm), lambda i: (0, i)),
            pl.BlockSpec((n_slab, tm, RNN_BLOCK), lambda i: (0, i, 0)),
            pl.BlockSpec((tm, rnn_width), lambda i: (i, 0)),
            pl.BlockSpec((tm, gm_width), lambda i: (i, 0)),
        ],
        out_shape=[
            jax.ShapeDtypeStruct((m, q_width), BF16),
            jax.ShapeDtypeStruct((m, kv_width), BF16),
            jax.ShapeDtypeStruct((kv_width, m), BF16),
            jax.ShapeDtypeStruct((n_slab, m, RNN_BLOCK), F32),
            jax.ShapeDtypeStruct((m, rnn_width), F32),
            jax.ShapeDtypeStruct((m, gm_width), F32),
        ],
        compiler_params=pltpu.CompilerParams(
            dimension_semantics=("arbitrary",), vmem_limit_bytes=VMEM_LIMIT_BYTES),
        name="proj",
    )(x2d, g1, w_in, qg, kg, cos, sin)


ATTN_KEY_CHUNK = 1024
ATTN_MAX_RAW_SCORE = 50.0

SCAN_SEGMENTS = 48
GATE_BIAS_ROWS = 2


def _rnn_stages(xr_ref, xm_ref, cw_ref, cb_ref, w4_ref, lam_ref, o_ref,
                xs_ref, a_refs, u_refs, h_refs, *, seq):
    t_all = N_META + seq
    seg = t_all // SCAN_SEGMENTS
    n_vreg = SCAN_SEGMENTS // SUBLANES
    pad = SUBLANES
    chains = [(d, v) for d in range(2) for v in range(n_vreg)]
    live = {}

    def rows(d, jj, v):
        j = jj if d == 0 else seg - 1 - jj
        return pl.ds(j + v * SUBLANES * seg, SUBLANES, stride=seg)

    def conv():
        xs_ref[0:pad, :] = jnp.zeros((pad, LANES), F32)
        xs_ref[pad:pad + N_META, :] = xm_ref[0]
        xs_ref[pad + N_META:pad + t_all, :] = xr_ref[0]
        xs_ref[pad + t_all:pad + t_all + pad, :] = jnp.zeros((pad, LANES), F32)
        xh = cb_ref[...]
        for j in range(CONV_W):
            off = pad + j - CONV_PAD_L
            xh = xh + xs_ref[off:off + t_all, :] * cw_ref[j:j + 1, :]
        live["xh"] = xh

    def gate_matmul():
        lane = lax.broadcasted_iota(jnp.int32, (t_all, LANES), 1)
        ones_cols = jnp.where(lane < GATE_BIAS_ROWS, 1.0, 0.0).astype(BF16)
        live["zh"] = jnp.dot(jnp.concatenate([live["xh"].astype(BF16), ones_cols], axis=1), w4_ref[0],
                             preferred_element_type=F32)

    def decay_and_input(d):
        zh, xh = live["zh"], live["xh"]
        t_r = jnp.tanh(zh[:, (2 * d) * LANES:(2 * d + 1) * LANES])
        t_i = jnp.tanh(zh[:, (2 * d + 1) * LANES:(2 * d + 2) * LANES])
        c_half = (-0.5 * RG_C) * jax.nn.log_sigmoid(lam_ref[d:d + 1, :])
        neg_log_a = c_half * t_r + c_half
        a = jnp.exp2(neg_log_a * (-LOG2_E))
        a_refs[d][...] = a
        y = jnp.tanh(neg_log_a) * (a * a + 1.0)
        root = y * lax.rsqrt(jnp.maximum(y, F32_TINY))
        u_refs[d][...] = root * (t_i * xh + xh)

    def segment_end_states():
        carry = [(jnp.zeros((SUBLANES, LANES), F32), jnp.ones((SUBLANES, LANES), F32))] * len(chains)
        for jj in range(seg):
            new = []
            for (d, v), (h, p) in zip(chains, carry):
                av = a_refs[d][rows(d, jj, v), :]
                new.append((av * h + u_refs[d][rows(d, jj, v), :], av * p))
            carry = new
        start = {}
        for d in range(2):
            order = range(SCAN_SEGMENTS) if d == 0 else range(SCAN_SEGMENTS - 1, -1, -1)
            c = jnp.zeros((1, LANES), F32)
            cs = [None] * SCAN_SEGMENTS
            for s in order:
                cs[s] = c
                h_end, p_end = carry[chains.index((d, s // SUBLANES))]
                r = s % SUBLANES
                c = p_end[r:r + 1, :] * c + h_end[r:r + 1, :]
            for v in range(n_vreg):
                start[(d, v)] = jnp.concatenate(cs[v * SUBLANES:(v + 1) * SUBLANES], axis=0)
        live["start"] = start

    def states_and_output():
        carry = [live["start"][c] for c in chains]
        for jj in range(seg):
            new = []
            for (d, v), h in zip(chains, carry):
                h = a_refs[d][rows(d, jj, v), :] * h + u_refs[d][rows(d, jj, v), :]
                h_refs[d][rows(d, jj, v), :] = h
                new.append(h)
            carry = new
        o_ref[...] = h_refs[0][N_META:, :] + h_refs[1][N_META:, :]

    return [conv, gate_matmul, functools.partial(decay_and_input, 0),
            functools.partial(decay_and_input, 1), segment_end_states, states_and_output]


def _rnn_kernel(xr_ref, xm_ref, cw_ref, cb_ref, w4_ref, lam_ref, o_ref,
                xs_ref, a0_ref, a1_ref, u0_ref, u1_ref, h0_ref, h1_ref, *, seq):
    for stage in _rnn_stages(xr_ref, xm_ref, cw_ref, cb_ref, w4_ref, lam_ref, o_ref, xs_ref,
                             (a0_ref, a1_ref), (u0_ref, u1_ref), (h0_ref, h1_ref), seq=seq):
        stage()


def _attn_kernel(raw_ok_ref, q_ref, k_ref, vt_ref, km_ref, vtm_ref, o_ref, *, bq):
    nt = (((1,), (1,)), ((), ()))
    seq = k_ref.shape[0]
    n_chunks = seq // ATTN_KEY_CHUNK
    chunk = lambda c: slice(c * ATTN_KEY_CHUNK, (c + 1) * ATTN_KEY_CHUNK)

    def attend(running_maximum):
        q = q_ref[...]
        qs = jnp.concatenate([q[:, g * HEAD_DIM:(g + 1) * HEAD_DIM] for g in range(Q_GROUP)], axis=0)
        scores = lambda c: lax.dot_general(k_ref[chunk(c), :], qs, nt, preferred_element_type=F32)

        sm = lax.dot_general(km_ref[...], qs, nt, preferred_element_type=F32)
        m = jnp.max(sm, axis=0, keepdims=True) if running_maximum else None
        pm = jnp.exp2(sm - m) if running_maximum else jnp.exp2(sm)
        l = jnp.sum(pm, axis=0, keepdims=True)
        ot = jnp.dot(vtm_ref[...], pm.astype(BF16), preferred_element_type=F32)
        s_next = scores(0)
        for c in range(n_chunks):
            s = s_next
            if c + 1 < n_chunks:
                s_next = scores(c + 1)
            if running_maximum:
                m_new = jnp.maximum(m, jnp.max(s, axis=0, keepdims=True))
                alpha = jnp.exp2(m - m_new)
                p = jnp.exp2(s - m_new)
                l = alpha * l + jnp.sum(p, axis=0, keepdims=True)
                ot = alpha * ot
                m = m_new
            else:
                p = jnp.exp2(s)
                l = l + jnp.sum(p, axis=0, keepdims=True)
            ot = ot + jnp.dot(vt_ref[:, chunk(c)], p.astype(BF16), preferred_element_type=F32)
        ot = ot * (ATTN_OUT_SCALE / l)
        for g in range(Q_GROUP):
            o_ref[:, g * HEAD_DIM:(g + 1) * HEAD_DIM] = ot[:, g * bq:(g + 1) * bq].T.astype(BF16)

    @pl.when(raw_ok_ref[0] == 1)
    def _raw_scores():
        attend(running_maximum=False)

    @pl.when(raw_ok_ref[0] != 1)
    def _running_maximum():
        attend(running_maximum=True)


def _attn(raw_ok, q, k, vt, k_meta, vt_meta, *, batch, seq, bq):
    m = q.shape[0]
    nq = seq // bq
    gw = Q_GROUP * HEAD_DIM
    assert seq % ATTN_KEY_CHUNK == 0
    kern = functools.partial(_attn_kernel, bq=bq)
    return pl.pallas_call(
        kern,
        grid=(batch, N_KV_HEADS, nq),
        in_specs=[
            pl.BlockSpec(memory_space=pltpu.SMEM),
            pl.BlockSpec((bq, gw), lambda b, h, i: (b * nq + i, h)),
            pl.BlockSpec((seq, HEAD_DIM), lambda b, h, i: (b, h)),
            pl.BlockSpec((HEAD_DIM, seq), lambda b, h, i: (h, b)),
            pl.BlockSpec((N_META, HEAD_DIM), lambda b, h, i: (0, h)),
            pl.BlockSpec((HEAD_DIM, N_META), lambda b, h, i: (h, 0)),
        ],
        out_specs=pl.BlockSpec((bq, gw), lambda b, h, i: (b * nq + i, h)),
        out_shape=jax.ShapeDtypeStruct((m, N_Q_HEADS * HEAD_DIM), BF16),
        compiler_params=pltpu.CompilerParams(
            dimension_semantics=("arbitrary", "arbitrary", "arbitrary"),
            vmem_limit_bytes=VMEM_LIMIT_BYTES),
        name="attn",
    )(raw_ok, q, k, vt, k_meta, vt_meta)


def _rnn(xr, xr_meta, conv_w, conv_b, w4, lam, *, batch, seq):
    n_slab, m, _ = xr.shape
    t_all = N_META + seq
    assert t_all % SCAN_SEGMENTS == 0
    kern = functools.partial(_rnn_kernel, seq=seq)
    return pl.pallas_call(
        kern,
        grid=(batch, n_slab),
        in_specs=[
            pl.BlockSpec((1, seq, LANES), lambda b, n: (n, b, 0)),
            pl.BlockSpec((1, N_META, LANES), lambda b, n: (n, 0, 0)),
            pl.BlockSpec((CONV_W, LANES), lambda b, n: (0, n)),
            pl.BlockSpec((1, LANES), lambda b, n: (0, n)),
            pl.BlockSpec((1, 2 * LANES, 4 * LANES), lambda b, n: (n, 0, 0)),
            pl.BlockSpec((2, LANES), lambda b, n: (0, n)),
        ],
        out_specs=pl.BlockSpec((seq, LANES), lambda b, n: (b, n)),
        out_shape=jax.ShapeDtypeStruct((m, n_slab * LANES), F32),
        scratch_shapes=[
            pltpu.VMEM((t_all + 2 * SUBLANES, LANES), F32),
        ] + [pltpu.VMEM((t_all, LANES), F32)] * 6,
        compiler_params=pltpu.CompilerParams(
            dimension_semantics=("arbitrary", "arbitrary"), vmem_limit_bytes=VMEM_LIMIT_BYTES),
        name="rnn",
    )(xr, xr_meta, conv_w, conv_b, w4, lam)


MXU_TILE = 256
POST_FFN_CHUNK = 512


def _ffn_chunks(d_ff):
    first = d_ff % POST_FFN_CHUNK or POST_FFN_CHUNK
    assert first % MXU_TILE == 0
    bounds = [0, first] + list(range(first + POST_FFN_CHUNK, d_ff + 1, POST_FFN_CHUNK))
    return list(zip(bounds[:-1], bounds[1:]))


def _post_kernel(x_ref, attn_ref, rnn_ref, gr_ref, gm_ref, wo_ref, g2_ref, wi_ref, wf_ref, o_ref,
                 *, d_model, d_ff):
    t_attn = jnp.tanh(gm_ref[:, :d_model])
    t_rnn = jnp.tanh(gm_ref[:, d_model:])
    xq = gr_ref[...]
    t_gelu = jnp.tanh(xq * (GELU_K1 + GELU_K3 * (xq * xq)))
    half_rnn = rnn_ref[...] * (xq * t_gelu + xq)
    half_attn = attn_ref[...].astype(F32)
    mix = (t_attn * half_attn + half_attn) + (t_rnn * half_rnn + half_rnn)
    h1 = x_ref[...] + jnp.dot(mix.astype(BF16), wo_ref[...], preferred_element_type=F32)
    ms = jnp.mean(h1 * h1, axis=-1, keepdims=True)
    hn = (h1 * lax.rsqrt(ms + EPS) * g2_ref[...]).astype(BF16)
    out = h1
    for c0, c1 in _ffn_chunks(d_ff):
        g = jnp.dot(hn, wi_ref[:, c0:c1], preferred_element_type=F32)
        u = jnp.dot(hn, wi_ref[:, d_ff + c0:d_ff + c1], preferred_element_type=F32)
        act = (g * _sigmoid(g) * u).astype(BF16)
        out = out + jnp.dot(act, wf_ref[c0:c1, :], preferred_element_type=F32)
    o_ref[...] = out


def _post(x2d, attn, rnn, gr, gm, w_out, g2, w_ffn_in, w_ffn_out, *, tm):
    m, d = x2d.shape
    d_ff = w_ffn_out.shape[0]
    kern = functools.partial(_post_kernel, d_model=d, d_ff=d_ff)
    row = lambda w: pl.BlockSpec((tm, w), lambda i: (i, 0))
    return pl.pallas_call(
        kern,
        grid=(m // tm,),
        in_specs=[row(d), row(d), row(d), row(d), row(2 * d),
                  _resident((d, d)), _resident((1, d)), _resident((d, 2 * d_ff)), _resident((d_ff, d))],
        out_specs=row(d),
        out_shape=jax.ShapeDtypeStruct((m, d), F32),
        compiler_params=pltpu.CompilerParams(
            dimension_semantics=("arbitrary",), vmem_limit_bytes=VMEM_LIMIT_BYTES),
        name="post",
    )(x2d, attn, rnn, gr, gm, w_out, g2, w_ffn_in, w_ffn_out)


ROW_TILE = 512
ATTN_QUERY_BLOCK = 512


def _rope_tables(seq):
    pos = jnp.arange(seq, dtype=jnp.int32)
    row = (pos // GRID_W).astype(F32)
    col = (pos % GRID_W).astype(F32)
    inv_freq = jnp.exp(-jnp.log(jnp.float32(ROPE_THETA)) * jnp.arange(ROPE_PAIRS, dtype=F32) / ROPE_PAIRS)
    ang_r = row[:, None] * inv_freq[None, :]
    ang_c = col[:, None] * inv_freq[None, :]
    cos = jnp.concatenate([jnp.cos(ang_r)] * 2 + [jnp.cos(ang_c)] * 2, axis=-1)
    sin = jnp.concatenate([-jnp.sin(ang_r), jnp.sin(ang_r), -jnp.sin(ang_c), jnp.sin(ang_c)], axis=-1)
    return cos, sin


def kernel(x, meta_tokens, norm1_g, w_in, conv_w, conv_b, rg_wa, rg_ba, rg_wx, rg_bx, rg_lambda,
           q_norm_g, k_norm_g, w_out, norm2_g, w_ffn_in, w_ffn_out):
    batch, seq, d = x.shape
    depth = norm1_g.shape[0]
    assert depth == 1, "meta-token rows are only skipped because no later layer reads them"
    q_width = N_Q_HEADS * HEAD_DIM
    kv_width = N_KV_HEADS * HEAD_DIM
    rnn_width = conv_w.shape[-1]
    n_slab = rnn_width // RNN_BLOCK
    assert meta_tokens.shape == (N_META, d) and seq % GRID_W == 0 and seq % ROW_TILE == 0

    x2d = x.reshape(batch * seq, d)
    g1 = norm1_g[0].reshape(1, d)
    gate_width = w_in.shape[-1] - q_width - 2 * kv_width - 2 * rnn_width
    col_scale = jnp.concatenate([jnp.ones((q_width + 2 * kv_width + rnn_width,), F32),
                                 jnp.full((rnn_width,), GELU_INPUT_SCALE, F32),
                                 jnp.full((gate_width,), GATE_INPUT_SCALE, F32)])
    w_in_b = (w_in[0] * col_scale).astype(BF16)
    qg = q_norm_g[0].reshape(1, HEAD_DIM)
    kg = k_norm_g[0].reshape(1, HEAD_DIM)
    cos, sin = _rope_tables(seq)
    cos_meta = jnp.ones((N_META, HEAD_DIM), F32)
    sin_meta = jnp.zeros((N_META, HEAD_DIM), F32)

    proj = functools.partial(_proj, q_width=q_width, kv_width=kv_width, rnn_width=rnn_width)
    q, k, vt, xr, gr, gm = proj(x2d, g1, w_in_b, qg, kg, cos, sin, tm=ROW_TILE)
    _, k_meta, vt_meta, xr_meta, _, _ = proj(
        meta_tokens.astype(x.dtype), g1, w_in_b, qg, kg, cos_meta, sin_meta, tm=N_META)

    score_bound = (HEAD_DIM * HEAD_DIM ** -0.5 * LOG2_E) * jnp.max(jnp.abs(qg)) * jnp.max(jnp.abs(kg))
    raw_ok = (score_bound <= ATTN_MAX_RAW_SCORE).astype(jnp.int32).reshape(1)

    w4 = jnp.concatenate([rg_wa[0, 0], rg_wx[0, 0], rg_wa[0, 1], rg_wx[0, 1]], axis=-1)
    b4 = 0.5 * jnp.stack([rg_ba[0, 0], rg_bx[0, 0], rg_ba[0, 1], rg_bx[0, 1]], axis=0)
    b4 = b4.reshape(4, n_slab, RNN_BLOCK).transpose(1, 0, 2).reshape(n_slab, 1, 4 * RNN_BLOCK)
    b4_hi = b4.astype(BF16)
    b4_lo = (b4 - b4_hi.astype(F32)).astype(BF16)
    w4_pad = jnp.zeros((n_slab, RNN_BLOCK - GATE_BIAS_ROWS, 4 * RNN_BLOCK), BF16)
    w4 = jnp.concatenate([w4.astype(BF16), b4_hi, b4_lo, w4_pad], axis=1)
    attn = _attn(raw_ok, q, k, vt, k_meta, vt_meta, batch=batch, seq=seq, bq=ATTN_QUERY_BLOCK)
    rnn = _rnn(xr, xr_meta, 0.5 * conv_w[0], 0.5 * conv_b[0].reshape(1, rnn_width), w4, rg_lambda[0],
               batch=batch, seq=seq)

    out = _post(x2d, attn, rnn, gr, gm, w_out[0].astype(BF16), norm2_g[0].reshape(1, d),
                w_ffn_in[0].astype(BF16), w_ffn_out[0].astype(BF16), tm=ROW_TILE)
    return out.reshape(batch, seq, d)
```

```python
import functools

import jax
import jax.numpy as jnp
from jax import lax
from jax.experimental import pallas as pl
from jax.experimental.pallas import tpu as pltpu

N_META = 16
GRID_W = 64
HEAD_DIM = 128
N_Q_HEADS = 8
N_KV_HEADS = 2
Q_GROUP = N_Q_HEADS // N_KV_HEADS
RNN_BLOCK = 128
CONV_W = 4
CONV_PAD_L = CONV_W // 2
RG_C = 8.0
ROPE_THETA = 10000.0
ROPE_PAIRS = HEAD_DIM // 4
EPS = 1e-6
LOG2_E = 1.4426950408889634
F32_TINY = 1.1754943508222875e-38

GATE_INPUT_SCALE = 0.5
GELU_INPUT_SCALE = 0.25
ATTN_OUT_SCALE = 0.5
GELU_K1 = 0.7978845608028654 / GELU_INPUT_SCALE
GELU_K3 = 0.7978845608028654 * 0.044715 / GELU_INPUT_SCALE ** 3

LANES = 128
SUBLANES = 8
VMEM_LIMIT_BYTES = 56 * 1024 * 1024

BF16 = jnp.bfloat16
F32 = jnp.float32


def _sigmoid(x):
    return 0.5 * jnp.tanh(0.5 * x) + 0.5


def _resident(shape):
    nd = len(shape)
    return pl.BlockSpec(shape, lambda *_: (0,) * nd, pipeline_mode=pl.Buffered(1))


def _swap_halves(x):
    lane = lax.broadcasted_iota(jnp.int32, x.shape, 1)
    lo = (lane % (2 * ROPE_PAIRS)) < ROPE_PAIRS
    return jnp.where(lo, pltpu.roll(x, LANES - ROPE_PAIRS, 1), pltpu.roll(x, ROPE_PAIRS, 1))


def _head_norm_rope(xh, g, cos, sin):
    ms = jnp.mean(xh * xh, axis=-1, keepdims=True)
    xn = xh * lax.rsqrt(ms + EPS) * g
    return xn * cos + _swap_halves(xn) * sin


def _proj_kernel(x_ref, g1_ref, w_ref, qg_ref, kg_ref, cos_ref, sin_ref,
                 q_ref, k_ref, vt_ref, xr_ref, gr_ref, gm_ref,
                 *, d_model, q_width, kv_width, rnn_width):
    x = x_ref[...]
    ms = jnp.mean(x * x, axis=-1, keepdims=True)
    xn = (x * lax.rsqrt(ms + EPS) * g1_ref[...]).astype(BF16)

    cos = cos_ref[...]
    sin = sin_ref[...]
    scale = HEAD_DIM ** -0.5 * LOG2_E

    o_k = q_width
    o_v = o_k + kv_width
    o_xr = o_v + kv_width
    o_gr = o_xr + rnn_width
    o_gm = o_gr + rnn_width

    project = lambda c0, c1: jnp.dot(xn, w_ref[:, c0:c1], preferred_element_type=F32)

    pq = project(0, o_k)
    for h in range(q_width // HEAD_DIM):
        sl = slice(h * HEAD_DIM, (h + 1) * HEAD_DIM)
        qh = _head_norm_rope(pq[:, sl], qg_ref[...], cos, sin)
        q_ref[:, sl] = (qh * scale).astype(BF16)

    pkv = project(o_k, o_xr)
    for h in range(kv_width // HEAD_DIM):
        sl = slice(h * HEAD_DIM, (h + 1) * HEAD_DIM)
        kh = _head_norm_rope(pkv[:, sl], kg_ref[...], cos, sin)
        k_ref[:, sl] = kh.astype(BF16)
    vt_ref[...] = pkv[:, kv_width:].T.astype(BF16)

    pxr = project(o_xr, o_gr)
    for n in range(rnn_width // RNN_BLOCK):
        xr_ref[n] = pxr[:, n * RNN_BLOCK:(n + 1) * RNN_BLOCK]

    gr_ref[...] = project(o_gr, o_gm)
    gm_ref[...] = project(o_gm, w_ref.shape[1])


def _proj(x2d, g1, w_in, qg, kg, cos, sin, *, tm, q_width, kv_width, rnn_width):
    m, d = x2d.shape
    in_width = w_in.shape[1]
    gm_width = in_width - q_width - 2 * kv_width - 2 * rnn_width
    n_tab = cos.shape[0] // tm
    n_slab = rnn_width // RNN_BLOCK
    kern = functools.partial(_proj_kernel, d_model=d, q_width=q_width, kv_width=kv_width,
                             rnn_width=rnn_width)
    return pl.pallas_call(
        kern,
        grid=(m // tm,),
        in_specs=[
            pl.BlockSpec((tm, d), lambda i: (i, 0)),
            _resident((1, d)),
            _resident((d, in_width)),
            _resident((1, HEAD_DIM)),
            _resident((1, HEAD_DIM)),
            pl.BlockSpec((tm, HEAD_DIM), lambda i: (i % n_tab, 0)),
            pl.BlockSpec((tm, HEAD_DIM), lambda i: (i % n_tab, 0)),
        ],
        out_specs=[
            pl.BlockSpec((tm, q_width), lambda i: (i, 0)),
            pl.BlockSpec((tm, kv_width), lambda i: (i, 0)),
            pl.BlockSpec((kv_width, tm), lambda i: (0, i)),
            pl.BlockSpec((n_slab, tm, RNN_BLOCK), lambda i: (0, i, 0)),
            pl.BlockSpec((tm, rnn_width), lambda i: (i, 0)),
            pl.BlockSpec((tm, gm_width), lambda i: (i, 0)),
        ],
        out_shape=[
            jax.ShapeDtypeStruct((m, q_width), BF16),
            jax.ShapeDtypeStruct((m, kv_width), BF16),
            jax.ShapeDtypeStruct((kv_width, m), BF16),
            jax.ShapeDtypeStruct((n_slab, m, RNN_BLOCK), F32),
            jax.ShapeDtypeStruct((m, rnn_width), F32),
            jax.ShapeDtypeStruct((m, gm_width), F32),
        ],
        compiler_params=pltpu.CompilerParams(
            dimension_semantics=("arbitrary",), vmem_limit_bytes=VMEM_LIMIT_BYTES),
        name="proj",
    )(x2d, g1, w_in, qg, kg, cos, sin)


ATTN_KEY_CHUNK = 1024
ATTN_MAX_RAW_SCORE = 50.0

SCAN_SEGMENTS = 48
GATE_BIAS_ROWS = 2


def _rnn_stages(xr_ref, xm_ref, cw_ref, cb_ref, w4_ref, lam_ref, o_ref,
                xs_ref, a_refs, u_refs, h_refs, *, seq):
    t_all = N_META + seq
    seg = t_all // SCAN_SEGMENTS
    n_vreg = SCAN_SEGMENTS // SUBLANES
    pad = SUBLANES
    chains = [(d, v) for d in range(2) for v in range(n_vreg)]
    live = {}

    def rows(d, jj, v):
        j = jj if d == 0 else seg - 1 - jj
        return pl.ds(j + v * SUBLANES * seg, SUBLANES, stride=seg)

    def conv():
        xs_ref[0:pad, :] = jnp.zeros((pad, LANES), F32)
        xs_ref[pad:pad + N_META, :] = xm_ref[0]
        xs_ref[pad + N_META:pad + t_all, :] = xr_ref[0]
        xs_ref[pad + t_all:pad + t_all + pad, :] = jnp.zeros((pad, LANES), F32)
        xh = cb_ref[...]
        for j in range(CONV_W):
            off = pad + j - CONV_PAD_L
            xh = xh + xs_ref[off:off + t_all, :] * cw_ref[j:j + 1, :]
        live["xh"] = xh

    def gate_matmul():
        lane = lax.broadcasted_iota(jnp.int32, (t_all, LANES), 1)
        ones_cols = jnp.where(lane < GATE_BIAS_ROWS, 1.0, 0.0).astype(BF16)
        live["zh"] = jnp.dot(jnp.concatenate([live["xh"].astype(BF16), ones_cols], axis=1), w4_ref[0],
                             preferred_element_type=F32)

    def decay_and_input(d):
        zh, xh = live["zh"], live["xh"]
        t_r = jnp.tanh(zh[:, (2 * d) * LANES:(2 * d + 1) * LANES])
        t_i = jnp.tanh(zh[:, (2 * d + 1) * LANES:(2 * d + 2) * LANES])
        c_half = (-0.5 * RG_C) * jax.nn.log_sigmoid(lam_ref[d:d + 1, :])
        neg_log_a = c_half * t_r + c_half
        a = jnp.exp2(neg_log_a * (-LOG2_E))
        a_refs[d][...] = a
        y = jnp.tanh(neg_log_a) * (a * a + 1.0)
        root = y * lax.rsqrt(jnp.maximum(y, F32_TINY))
        u_refs[d][...] = root * (t_i * xh + xh)

    def segment_end_states():
        carry = [(jnp.zeros((SUBLANES, LANES), F32), jnp.ones((SUBLANES, LANES), F32))] * len(chains)
        for jj in range(seg):
            new = []
            for (d, v), (h, p) in zip(chains, carry):
                av = a_refs[d][rows(d, jj, v), :]
                new.append((av * h + u_refs[d][rows(d, jj, v), :], av * p))
            carry = new
        start = {}
        for d in range(2):
            order = range(SCAN_SEGMENTS) if d == 0 else range(SCAN_SEGMENTS - 1, -1, -1)
            c = jnp.zeros((1, LANES), F32)
            cs = [None] * SCAN_SEGMENTS
            for s in order:
                cs[s] = c
                h_end, p_end = carry[chains.index((d, s // SUBLANES))]
                r = s % SUBLANES
                c = p_end[r:r + 1, :] * c + h_end[r:r + 1, :]
            for v in range(n_vreg):
                start[(d, v)] = jnp.concatenate(cs[v * SUBLANES:(v + 1) * SUBLANES], axis=0)
        live["start"] = start

    def states_and_output():
        carry = [live["start"][c] for c in chains]
        for jj in range(seg):
            new = []
            for (d, v), h in zip(chains, carry):
                h = a_refs[d][rows(d, jj, v), :] * h + u_refs[d][rows(d, jj, v), :]
                h_refs[d][rows(d, jj, v), :] = h
                new.append(h)
            carry = new
        o_ref[...] = h_refs[0][N_META:, :] + h_refs[1][N_META:, :]

    return [conv, gate_matmul, functools.partial(decay_and_input, 0),
            functools.partial(decay_and_input, 1), segment_end_states, states_and_output]


def _rnn_kernel(xr_ref, xm_ref, cw_ref, cb_ref, w4_ref, lam_ref, o_ref,
                xs_ref, a0_ref, a1_ref, u0_ref, u1_ref, h0_ref, h1_ref, *, seq):
    for stage in _rnn_stages(xr_ref, xm_ref, cw_ref, cb_ref, w4_ref, lam_ref, o_ref, xs_ref,
                             (a0_ref, a1_ref), (u0_ref, u1_ref), (h0_ref, h1_ref), seq=seq):
        stage()


def _attn_kernel(raw_ok_ref, q_ref, k_ref, vt_ref, km_ref, vtm_ref, o_ref, *, bq):
    nt = (((1,), (1,)), ((), ()))
    seq = k_ref.shape[0]
    n_chunks = seq // ATTN_KEY_CHUNK
    chunk = lambda c: slice(c * ATTN_KEY_CHUNK, (c + 1) * ATTN_KEY_CHUNK)

    def attend(running_maximum):
        q = q_ref[...]
        qs = jnp.concatenate([q[:, g * HEAD_DIM:(g + 1) * HEAD_DIM] for g in range(Q_GROUP)], axis=0)
        scores = lambda c: lax.dot_general(k_ref[chunk(c), :], qs, nt, preferred_element_type=F32)

        sm = lax.dot_general(km_ref[...], qs, nt, preferred_element_type=F32)
        m = jnp.max(sm, axis=0, keepdims=True) if running_maximum else None
        pm = jnp.exp2(sm - m) if running_maximum else jnp.exp2(sm)
        l = jnp.sum(pm, axis=0, keepdims=True)
        ot = jnp.dot(vtm_ref[...], pm.astype(BF16), preferred_element_type=F32)
        s_next = scores(0)
        for c in range(n_chunks):
            s = s_next
            if c + 1 < n_chunks:
                s_next = scores(c + 1)
            if running_maximum:
                m_new = jnp.maximum(m, jnp.max(s, axis=0, keepdims=True))
                alpha = jnp.exp2(m - m_new)
                p = jnp.exp2(s - m_new)
                l = alpha * l + jnp.sum(p, axis=0, keepdims=True)
                ot = alpha * ot
                m = m_new
            else:
                p = jnp.exp2(s)
                l = l + jnp.sum(p, axis=0, keepdims=True)
            ot = ot + jnp.dot(vt_ref[:, chunk(c)], p.astype(BF16), preferred_element_type=F32)
        ot = ot * (ATTN_OUT_SCALE / l)
        for g in range(Q_GROUP):
            o_ref[:, g * HEAD_DIM:(g + 1) * HEAD_DIM] = ot[:, g * bq:(g + 1) * bq].T.astype(BF16)

    @pl.when(raw_ok_ref[0] == 1)
    def _raw_scores():
        attend(running_maximum=False)

    @pl.when(raw_ok_ref[0] != 1)
    def _running_maximum():
        attend(running_maximum=True)


def _attn(raw_ok, q, k, vt, k_meta, vt_meta, *, batch, seq, bq):
    m = q.shape[0]
    nq = seq // bq
    gw = Q_GROUP * HEAD_DIM
    assert seq % ATTN_KEY_CHUNK == 0
    kern = functools.partial(_attn_kernel, bq=bq)
    return pl.pallas_call(
        kern,
        grid=(batch, N_KV_HEADS, nq),
        in_specs=[
            pl.BlockSpec(memory_space=pltpu.SMEM),
            pl.BlockSpec((bq, gw), lambda b, h, i: (b * nq + i, h)),
            pl.BlockSpec((seq, HEAD_DIM), lambda b, h, i: (b, h)),
            pl.BlockSpec((HEAD_DIM, seq), lambda b, h, i: (h, b)),
            pl.BlockSpec((N_META, HEAD_DIM), lambda b, h, i: (0, h)),
            pl.BlockSpec((HEAD_DIM, N_META), lambda b, h, i: (h, 0)),
        ],
        out_specs=pl.BlockSpec((bq, gw), lambda b, h, i: (b * nq + i, h)),
        out_shape=jax.ShapeDtypeStruct((m, N_Q_HEADS * HEAD_DIM), BF16),
        compiler_params=pltpu.CompilerParams(
            dimension_semantics=("arbitrary", "arbitrary", "arbitrary"),
            vmem_limit_bytes=VMEM_LIMIT_BYTES),
        name="attn",
    )(raw_ok, q, k, vt, k_meta, vt_meta)


def _rnn(xr, xr_meta, conv_w, conv_b, w4, lam, *, batch, seq):
    n_slab, m, _ = xr.shape
    t_all = N_META + seq
    assert t_all % SCAN_SEGMENTS == 0
    kern = functools.partial(_rnn_kernel, seq=seq)
    return pl.pallas_call(
        kern,
        grid=(batch, n_slab),
        in_specs=[
            pl.BlockSpec((1, seq, LANES), lambda b, n: (n, b, 0)),
            pl.BlockSpec((1, N_META, LANES), lambda b, n: (n, 0, 0)),
            pl.BlockSpec((CONV_W, LANES), lambda b, n: (0, n)),
            pl.BlockSpec((1, LANES), lambda b, n: (0, n)),
            pl.BlockSpec((1, 2 * LANES, 4 * LANES), lambda b, n: (n, 0, 0)),
            pl.BlockSpec((2, LANES), lambda b, n: (0, n)),
        ],
        out_specs=pl.BlockSpec((seq, LANES), lambda b, n: (b, n)),
        out_shape=jax.ShapeDtypeStruct((m, n_slab * LANES), F32),
        scratch_shapes=[
            pltpu.VMEM((t_all + 2 * SUBLANES, LANES), F32),
        ] + [pltpu.VMEM((t_all, LANES), F32)] * 6,
        compiler_params=pltpu.CompilerParams(
            dimension_semantics=("arbitrary", "arbitrary"), vmem_limit_bytes=VMEM_LIMIT_BYTES),
        name="rnn",
    )(xr, xr_meta, conv_w, conv_b, w4, lam)


MXU_TILE = 256
POST_FFN_CHUNK = 512
POST_MIX_CHUNK = 256


def _ffn_chunks(d_ff):
    first = d_ff % POST_FFN_CHUNK or POST_FFN_CHUNK
    assert first % MXU_TILE == 0
    bounds = [0, first] + list(range(first + POST_FFN_CHUNK, d_ff + 1, POST_FFN_CHUNK))
    return list(zip(bounds[:-1], bounds[1:]))


def _post_kernel(x_ref, attn_ref, rnn_ref, gr_ref, gm_ref, wo_ref, g2_ref, wi_ref, wf_ref, o_ref,
                 *, d_model, d_ff):
    mixed = None
    for c0 in range(0, d_model, POST_MIX_CHUNK):
        cols = slice(c0, c0 + POST_MIX_CHUNK)
        t_attn = jnp.tanh(gm_ref[:, cols])
        t_rnn = jnp.tanh(gm_ref[:, d_model + c0:d_model + c0 + POST_MIX_CHUNK])
        xq = gr_ref[:, cols]
        t_gelu = jnp.tanh(xq * (GELU_K1 + GELU_K3 * (xq * xq)))
        half_rnn = rnn_ref[:, cols] * (xq * t_gelu + xq)
        half_attn = attn_ref[:, cols].astype(F32)
        mix = (t_attn * half_attn + half_attn) + (t_rnn * half_rnn + half_rnn)
        part = jnp.dot(mix.astype(BF16), wo_ref[cols, :], preferred_element_type=F32)
        mixed = part if mixed is None else mixed + part
    h1 = x_ref[...] + mixed
    ms = jnp.mean(h1 * h1, axis=-1, keepdims=True)
    hn = (h1 * lax.rsqrt(ms + EPS) * g2_ref[...]).astype(BF16)
    out = h1
    for c0, c1 in _ffn_chunks(d_ff):
        g = jnp.dot(hn, wi_ref[:, c0:c1], preferred_element_type=F32)
        u = jnp.dot(hn, wi_ref[:, d_ff + c0:d_ff + c1], preferred_element_type=F32)
        act = (g * _sigmoid(g) * u).astype(BF16)
        out = out + jnp.dot(act, wf_ref[c0:c1, :], preferred_element_type=F32)
    o_ref[...] = out


def _post(x2d, attn, rnn, gr, gm, w_out, g2, w_ffn_in, w_ffn_out, *, tm):
    m, d = x2d.shape
    d_ff = w_ffn_out.shape[0]
    kern = functools.partial(_post_kernel, d_model=d, d_ff=d_ff)
    row = lambda w: pl.BlockSpec((tm, w), lambda i: (i, 0))
    return pl.pallas_call(
        kern,
        grid=(m // tm,),
        in_specs=[row(d), row(d), row(d), row(d), row(2 * d),
                  _resident((d, d)), _resident((1, d)), _resident((d, 2 * d_ff)), _resident((d_ff, d))],
        out_specs=row(d),
        out_shape=jax.ShapeDtypeStruct((m, d), F32),
        compiler_params=pltpu.CompilerParams(
            dimension_semantics=("arbitrary",), vmem_limit_bytes=VMEM_LIMIT_BYTES),
        name="post",
    )(x2d, attn, rnn, gr, gm, w_out, g2, w_ffn_in, w_ffn_out)


ROW_TILE = 512
ATTN_QUERY_BLOCK = 1024


def _rope_tables(seq):
    pos = jnp.arange(seq, dtype=jnp.int32)
    row = (pos // GRID_W).astype(F32)
    col = (pos % GRID_W).astype(F32)
    inv_freq = jnp.exp(-jnp.log(jnp.float32(ROPE_THETA)) * jnp.arange(ROPE_PAIRS, dtype=F32) / ROPE_PAIRS)
    ang_r = row[:, None] * inv_freq[None, :]
    ang_c = col[:, None] * inv_freq[None, :]
    cos = jnp.concatenate([jnp.cos(ang_r)] * 2 + [jnp.cos(ang_c)] * 2, axis=-1)
    sin = jnp.concatenate([-jnp.sin(ang_r), jnp.sin(ang_r), -jnp.sin(ang_c), jnp.sin(ang_c)], axis=-1)
    return cos, sin


def kernel(x, meta_tokens, norm1_g, w_in, conv_w, conv_b, rg_wa, rg_ba, rg_wx, rg_bx, rg_lambda,
           q_norm_g, k_norm_g, w_out, norm2_g, w_ffn_in, w_ffn_out):
    batch, seq, d = x.shape
    depth = norm1_g.shape[0]
    assert depth == 1, "meta-token rows are only skipped because no later layer reads them"
    q_width = N_Q_HEADS * HEAD_DIM
    kv_width = N_KV_HEADS * HEAD_DIM
    rnn_width = conv_w.shape[-1]
    n_slab = rnn_width // RNN_BLOCK
    assert meta_tokens.shape == (N_META, d) and seq % GRID_W == 0 and seq % ROW_TILE == 0

    x2d = x.reshape(batch * seq, d)
    g1 = norm1_g[0].reshape(1, d)
    gate_width = w_in.shape[-1] - q_width - 2 * kv_width - 2 * rnn_width
    col_scale = jnp.concatenate([jnp.ones((q_width + 2 * kv_width + rnn_width,), F32),
                                 jnp.full((rnn_width,), GELU_INPUT_SCALE, F32),
                                 jnp.full((gate_width,), GATE_INPUT_SCALE, F32)])
    w_in_b = (w_in[0] * col_scale).astype(BF16)
    qg = q_norm_g[0].reshape(1, HEAD_DIM)
    kg = k_norm_g[0].reshape(1, HEAD_DIM)
    cos, sin = _rope_tables(seq)
    cos_meta = jnp.ones((N_META, HEAD_DIM), F32)
    sin_meta = jnp.zeros((N_META, HEAD_DIM), F32)

    proj = functools.partial(_proj, q_width=q_width, kv_width=kv_width, rnn_width=rnn_width)
    q, k, vt, xr, gr, gm = proj(x2d, g1, w_in_b, qg, kg, cos, sin, tm=ROW_TILE)
    _, k_meta, vt_meta, xr_meta, _, _ = proj(
        meta_tokens.astype(x.dtype), g1, w_in_b, qg, kg, cos_meta, sin_meta, tm=N_META)

    score_bound = (HEAD_DIM * HEAD_DIM ** -0.5 * LOG2_E) * jnp.max(jnp.abs(qg)) * jnp.max(jnp.abs(kg))
    raw_ok = (score_bound <= ATTN_MAX_RAW_SCORE).astype(jnp.int32).reshape(1)

    w4 = jnp.concatenate([rg_wa[0, 0], rg_wx[0, 0], rg_wa[0, 1], rg_wx[0, 1]], axis=-1)
    b4 = 0.5 * jnp.stack([rg_ba[0, 0], rg_bx[0, 0], rg_ba[0, 1], rg_bx[0, 1]], axis=0)
    b4 = b4.reshape(4, n_slab, RNN_BLOCK).transpose(1, 0, 2).reshape(n_slab, 1, 4 * RNN_BLOCK)
    b4_hi = b4.astype(BF16)
    b4_lo = (b4 - b4_hi.astype(F32)).astype(BF16)
    w4_pad = jnp.zeros((n_slab, RNN_BLOCK - GATE_BIAS_ROWS, 4 * RNN_BLOCK), BF16)
    w4 = jnp.concatenate([w4.astype(BF16), b4_hi, b4_lo, w4_pad], axis=1)
    attn = _attn(raw_ok, q, k, vt, k_meta, vt_meta, batch=batch, seq=seq, bq=ATTN_QUERY_BLOCK)
    rnn = _rnn(xr, xr_meta, 0.5 * conv_w[0], 0.5 * conv_b[0].reshape(1, rnn_width), w4, rg_lambda[0],
               batch=batch, seq=seq)

    out = _post(x2d, attn, rnn, gr, gm, w_out[0].astype(BF16), norm2_g[0].reshape(1, d),
                w_ffn_in[0].astype(BF16), w_ffn_out[0].astype(BF16), tm=ROW_TILE)
    return out.reshape(batch, seq, d)
```

```python
import functools

import jax
import jax.numpy as jnp
from jax import lax
from jax.experimental import pallas as pl
from jax.experimental.pallas import tpu as pltpu

N_META = 16
GRID_W = 64
HEAD_DIM = 128
N_Q_HEADS = 8
N_KV_HEADS = 2
Q_GROUP = N_Q_HEADS // N_KV_HEADS
RNN_BLOCK = 128
CONV_W = 4
CONV_PAD_L = CONV_W // 2
RG_C = 8.0
ROPE_THETA = 10000.0
ROPE_PAIRS = HEAD_DIM // 4
EPS = 1e-6
LOG2_E = 1.4426950408889634
F32_TINY = 1.1754943508222875e-38

GATE_INPUT_SCALE = 0.5
GELU_INPUT_SCALE = 0.25
ATTN_OUT_SCALE = 0.5
GELU_K1 = 0.7978845608028654 / GELU_INPUT_SCALE
GELU_K3 = 0.7978845608028654 * 0.044715 / GELU_INPUT_SCALE ** 3

LANES = 128
SUBLANES = 8
VMEM_LIMIT_BYTES = 56 * 1024 * 1024

BF16 = jnp.bfloat16
F32 = jnp.float32


def _sigmoid(x):
    return 0.5 * jnp.tanh(0.5 * x) + 0.5


def _resident(shape):
    nd = len(shape)
    return pl.BlockSpec(shape, lambda *_: (0,) * nd, pipeline_mode=pl.Buffered(1))


def _swap_halves(x):
    lane = lax.broadcasted_iota(jnp.int32, x.shape, 1)
    lo = (lane % (2 * ROPE_PAIRS)) < ROPE_PAIRS
    return jnp.where(lo, pltpu.roll(x, LANES - ROPE_PAIRS, 1), pltpu.roll(x, ROPE_PAIRS, 1))


def _head_norm_rope(xh, g, cos, sin):
    ms = jnp.mean(xh * xh, axis=-1, keepdims=True)
    xn = xh * lax.rsqrt(ms + EPS) * g
    return xn * cos + _swap_halves(xn) * sin


def _proj_kernel(x_ref, g1_ref, w_ref, qg_ref, kg_ref, cos_ref, sin_ref,
                 q_ref, k_ref, vt_ref, xr_ref, gr_ref, gm_ref,
                 *, d_model, q_width, kv_width, rnn_width):
    x = x_ref[...]
    ms = jnp.mean(x * x, axis=-1, keepdims=True)
    xn = (x * lax.rsqrt(ms + EPS) * g1_ref[...]).astype(BF16)

    cos = cos_ref[...]
    sin = sin_ref[...]
    scale = HEAD_DIM ** -0.5 * LOG2_E

    o_k = q_width
    o_v = o_k + kv_width
    o_xr = o_v + kv_width
    o_gr = o_xr + rnn_width
    o_gm = o_gr + rnn_width

    project = lambda c0, c1: jnp.dot(xn, w_ref[:, c0:c1], preferred_element_type=F32)

    pq = project(0, o_k)
    for h in range(q_width // HEAD_DIM):
        sl = slice(h * HEAD_DIM, (h + 1) * HEAD_DIM)
        qh = _head_norm_rope(pq[:, sl], qg_ref[...], cos, sin)
        q_ref[:, sl] = (qh * scale).astype(BF16)

    pkv = project(o_k, o_xr)
    for h in range(kv_width // HEAD_DIM):
        sl = slice(h * HEAD_DIM, (h + 1) * HEAD_DIM)
        kh = _head_norm_rope(pkv[:, sl], kg_ref[...], cos, sin)
        k_ref[:, sl] = kh.astype(BF16)
    vt_ref[...] = pkv[:, kv_width:].T.astype(BF16)

    pxr = project(o_xr, o_gr)
    for n in range(rnn_width // RNN_BLOCK):
        xr_ref[n] = pxr[:, n * RNN_BLOCK:(n + 1) * RNN_BLOCK]

    gr_ref[...] = project(o_gr, o_gm)
    gm_ref[...] = project(o_gm, w_ref.shape[1])


def _proj(x2d, g1, w_in, qg, kg, cos, sin, *, tm, q_width, kv_width, rnn_width):
    m, d = x2d.shape
    in_width = w_in.shape[1]
    gm_width = in_width - q_width - 2 * kv_width - 2 * rnn_width
    n_tab = cos.shape[0] // tm
    n_slab = rnn_width // RNN_BLOCK
    kern = functools.partial(_proj_kernel, d_model=d, q_width=q_width, kv_width=kv_width,
                             rnn_width=rnn_width)
    return pl.pallas_call(
        kern,
        grid=(m // tm,),
        in_specs=[
            pl.BlockSpec((tm, d), lambda i: (i, 0)),
            _resident((1, d)),
            _resident((d, in_width)),
            _resident((1, HEAD_DIM)),
            _resident((1, HEAD_DIM)),
            pl.BlockSpec((tm, HEAD_DIM), lambda i: (i % n_tab, 0)),
            pl.BlockSpec((tm, HEAD_DIM), lambda i: (i % n_tab, 0)),
        ],
        out_specs=[
            pl.BlockSpec((tm, q_width), lambda i: (i, 0)),
            pl.BlockSpec((tm, kv_width), lambda i: (i, 0)),
            pl.BlockSpec((kv_width, tm), lambda i: (0, i)),
            pl.BlockSpec((n_slab, tm, RNN_BLOCK), lambda i: (0, i, 0)),
            pl.BlockSpec((tm, rnn_width), lambda i: (i, 0)),
            pl.BlockSpec((tm, gm_width), lambda i: (i, 0)),
        ],
        out_shape=[
            jax.ShapeDtypeStruct((m, q_width), BF16),
            jax.ShapeDtypeStruct((m, kv_width), BF16),
            jax.ShapeDtypeStruct((kv_width, m), BF16),
            jax.ShapeDtypeStruct((n_slab, m, RNN_BLOCK), F32),
            jax.ShapeDtypeStruct((m, rnn_width), F32),
            jax.ShapeDtypeStruct((m, gm_width), F32),
        ],
        compiler_params=pltpu.CompilerParams(
            dimension_semantics=("arbitrary",), vmem_limit_bytes=VMEM_LIMIT_BYTES),
        name="proj",
    )(x2d, g1, w_in, qg, kg, cos, sin)


ATTN_KEY_CHUNK = 1024
ATTN_MAX_RAW_SCORE = 50.0

SCAN_SEGMENTS = 48
GATE_BIAS_ROWS = 2
RNN_SLABS_PER_STEP = 4


def _rnn_stages(xr_ref, xm_ref, cw_ref, cb_ref, w4_ref, lam_ref, o_ref,
                xs_ref, a_refs, u_refs, h_refs, *, seq):
    t_all = N_META + seq
    seg = t_all // SCAN_SEGMENTS
    n_vreg = SCAN_SEGMENTS // SUBLANES
    pad = SUBLANES
    chains = [(d, v) for d in range(2) for v in range(n_vreg)]
    live = {}

    def rows(d, jj, v):
        j = jj if d == 0 else seg - 1 - jj
        return pl.ds(j + v * SUBLANES * seg, SUBLANES, stride=seg)

    def conv():
        xs_ref[0:pad, :] = jnp.zeros((pad, LANES), F32)
        xs_ref[pad:pad + N_META, :] = xm_ref[0]
        xs_ref[pad + N_META:pad + t_all, :] = xr_ref[0]
        xs_ref[pad + t_all:pad + t_all + pad, :] = jnp.zeros((pad, LANES), F32)
        xh = cb_ref[...]
        for j in range(CONV_W):
            off = pad + j - CONV_PAD_L
            xh = xh + xs_ref[off:off + t_all, :] * cw_ref[j:j + 1, :]
        live["xh"] = xh

    def gate_matmul():
        lane = lax.broadcasted_iota(jnp.int32, (t_all, LANES), 1)
        ones_cols = jnp.where(lane < GATE_BIAS_ROWS, 1.0, 0.0).astype(BF16)
        live["zh"] = jnp.dot(jnp.concatenate([live["xh"].astype(BF16), ones_cols], axis=1), w4_ref[0],
                             preferred_element_type=F32)

    def decay_and_input(d):
        zh, xh = live["zh"], live["xh"]
        t_r = jnp.tanh(zh[:, (2 * d) * LANES:(2 * d + 1) * LANES])
        t_i = jnp.tanh(zh[:, (2 * d + 1) * LANES:(2 * d + 2) * LANES])
        c_half = (-0.5 * RG_C) * jax.nn.log_sigmoid(lam_ref[d:d + 1, :])
        neg_log_a = c_half * t_r + c_half
        a = jnp.exp2(neg_log_a * (-LOG2_E))
        a_refs[d][...] = a
        y = jnp.tanh(neg_log_a) * (a * a + 1.0)
        root = y * lax.rsqrt(jnp.maximum(y, F32_TINY))
        u_refs[d][...] = root * (t_i * xh + xh)

    def segment_end_states():
        carry = [(jnp.zeros((SUBLANES, LANES), F32), jnp.ones((SUBLANES, LANES), F32))] * len(chains)
        for jj in range(seg):
            new = []
            for (d, v), (h, p) in zip(chains, carry):
                av = a_refs[d][rows(d, jj, v), :]
                new.append((av * h + u_refs[d][rows(d, jj, v), :], av * p))
            carry = new
        start = {}
        for d in range(2):
            order = range(SCAN_SEGMENTS) if d == 0 else range(SCAN_SEGMENTS - 1, -1, -1)
            c = jnp.zeros((1, LANES), F32)
            cs = [None] * SCAN_SEGMENTS
            for s in order:
                cs[s] = c
                h_end, p_end = carry[chains.index((d, s // SUBLANES))]
                r = s % SUBLANES
                c = p_end[r:r + 1, :] * c + h_end[r:r + 1, :]
            for v in range(n_vreg):
                start[(d, v)] = jnp.concatenate(cs[v * SUBLANES:(v + 1) * SUBLANES], axis=0)
        live["start"] = start

    def states_and_output():
        carry = [live["start"][c] for c in chains]
        for jj in range(seg):
            new = []
            for (d, v), h in zip(chains, carry):
                h = a_refs[d][rows(d, jj, v), :] * h + u_refs[d][rows(d, jj, v), :]
                h_refs[d][rows(d, jj, v), :] = h
                new.append(h)
            carry = new
        o_ref[...] = h_refs[0][N_META:, :] + h_refs[1][N_META:, :]

    return [conv, gate_matmul, functools.partial(decay_and_input, 0),
            functools.partial(decay_and_input, 1), segment_end_states, states_and_output]


def _rnn_kernel(xr_ref, xm_ref, cw_ref, cb_ref, w4_ref, lam_ref, o_ref, *scratch, seq):
    per_slab = len(scratch) // RNN_SLABS_PER_STEP
    slab_stages = []
    for s in range(RNN_SLABS_PER_STEP):
        xs_ref, a0_ref, a1_ref, u0_ref, u1_ref, h0_ref, h1_ref = scratch[s * per_slab:(s + 1) * per_slab]
        lanes = slice(s * LANES, (s + 1) * LANES)
        slab_stages.append(_rnn_stages(
            xr_ref.at[s:s + 1], xm_ref.at[s:s + 1], cw_ref.at[:, lanes], cb_ref.at[:, lanes],
            w4_ref.at[s:s + 1], lam_ref.at[:, lanes], o_ref.at[:, lanes], xs_ref,
            (a0_ref, a1_ref), (u0_ref, u1_ref), (h0_ref, h1_ref), seq=seq))
    for stages in zip(*slab_stages):
        for stage in stages:
            stage()


def _attn_kernel(raw_ok_ref, q_ref, k_ref, vt_ref, km_ref, vtm_ref, o_ref, *, bq):
    nt = (((1,), (1,)), ((), ()))
    seq = k_ref.shape[0]
    n_chunks = seq // ATTN_KEY_CHUNK
    chunk = lambda c: slice(c * ATTN_KEY_CHUNK, (c + 1) * ATTN_KEY_CHUNK)

    def attend(running_maximum):
        q = q_ref[...]
        qs = jnp.concatenate([q[:, g * HEAD_DIM:(g + 1) * HEAD_DIM] for g in range(Q_GROUP)], axis=0)
        scores = lambda c: lax.dot_general(k_ref[chunk(c), :], qs, nt, preferred_element_type=F32)

        sm = lax.dot_general(km_ref[...], qs, nt, preferred_element_type=F32)
        m = jnp.max(sm, axis=0, keepdims=True) if running_maximum else None
        pm = jnp.exp2(sm - m) if running_maximum else jnp.exp2(sm)
        l = jnp.sum(pm, axis=0, keepdims=True)
        ot = jnp.dot(vtm_ref[...], pm.astype(BF16), preferred_element_type=F32)
        s_next = scores(0)
        for c in range(n_chunks):
            s = s_next
            if c + 1 < n_chunks:
                s_next = scores(c + 1)
            if running_maximum:
                m_new = jnp.maximum(m, jnp.max(s, axis=0, keepdims=True))
                alpha = jnp.exp2(m - m_new)
                p = jnp.exp2(s - m_new)
                l = alpha * l + jnp.sum(p, axis=0, keepdims=True)
                ot = alpha * ot
                m = m_new
            else:
                p = jnp.exp2(s)
                l = l + jnp.sum(p, axis=0, keepdims=True)
            ot = ot + jnp.dot(vt_ref[:, chunk(c)], p.astype(BF16), preferred_element_type=F32)
        ot = ot * (ATTN_OUT_SCALE / l)
        for g in range(Q_GROUP):
            o_ref[:, g * HEAD_DIM:(g + 1) * HEAD_DIM] = ot[:, g * bq:(g + 1) * bq].T.astype(BF16)

    @pl.when(raw_ok_ref[0] == 1)
    def _raw_scores():
        attend(running_maximum=False)

    @pl.when(raw_ok_ref[0] != 1)
    def _running_maximum():
        attend(running_maximum=True)


def _attn(raw_ok, q, k, vt, k_meta, vt_meta, *, batch, seq, bq):
    m = q.shape[0]
    nq = seq // bq
    gw = Q_GROUP * HEAD_DIM
    assert seq % ATTN_KEY_CHUNK == 0
    kern = functools.partial(_attn_kernel, bq=bq)
    return pl.pallas_call(
        kern,
        grid=(batch, N_KV_HEADS, nq),
        in_specs=[
            pl.BlockSpec(memory_space=pltpu.SMEM),
            pl.BlockSpec((bq, gw), lambda b, h, i: (b * nq + i, h)),
            pl.BlockSpec((seq, HEAD_DIM), lambda b, h, i: (b, h)),
            pl.BlockSpec((HEAD_DIM, seq), lambda b, h, i: (h, b)),
            pl.BlockSpec((N_META, HEAD_DIM), lambda b, h, i: (0, h)),
            pl.BlockSpec((HEAD_DIM, N_META), lambda b, h, i: (h, 0)),
        ],
        out_specs=pl.BlockSpec((bq, gw), lambda b, h, i: (b * nq + i, h)),
        out_shape=jax.ShapeDtypeStruct((m, N_Q_HEADS * HEAD_DIM), BF16),
        compiler_params=pltpu.CompilerParams(
            dimension_semantics=("arbitrary", "arbitrary", "arbitrary"),
            vmem_limit_bytes=VMEM_LIMIT_BYTES),
        name="attn",
    )(raw_ok, q, k, vt, k_meta, vt_meta)


def _rnn(xr, xr_meta, conv_w, conv_b, w4, lam, *, batch, seq):
    n_slab, m, _ = xr.shape
    t_all = N_META + seq
    assert t_all % SCAN_SEGMENTS == 0
    kern = functools.partial(_rnn_kernel, seq=seq)
    g = RNN_SLABS_PER_STEP
    assert n_slab % g == 0
    return pl.pallas_call(
        kern,
        grid=(batch, n_slab // g),
        in_specs=[
            pl.BlockSpec((g, seq, LANES), lambda b, n: (n, b, 0)),
            pl.BlockSpec((g, N_META, LANES), lambda b, n: (n, 0, 0)),
            pl.BlockSpec((CONV_W, g * LANES), lambda b, n: (0, n)),
            pl.BlockSpec((1, g * LANES), lambda b, n: (0, n)),
            pl.BlockSpec((g, 2 * LANES, 4 * LANES), lambda b, n: (n, 0, 0)),
            pl.BlockSpec((2, g * LANES), lambda b, n: (0, n)),
        ],
        out_specs=pl.BlockSpec((seq, g * LANES), lambda b, n: (b, n)),
        out_shape=jax.ShapeDtypeStruct((m, n_slab * LANES), F32),
        scratch_shapes=([pltpu.VMEM((t_all + 2 * SUBLANES, LANES), F32)]
                        + [pltpu.VMEM((t_all, LANES), F32)] * 6) * g,
        compiler_params=pltpu.CompilerParams(
            dimension_semantics=("arbitrary", "arbitrary"), vmem_limit_bytes=VMEM_LIMIT_BYTES),
        name="rnn",
    )(xr, xr_meta, conv_w, conv_b, w4, lam)


MXU_TILE = 256
POST_FFN_CHUNK = 512
POST_MIX_CHUNK = 256


def _ffn_chunks(d_ff):
    first = d_ff % POST_FFN_CHUNK or POST_FFN_CHUNK
    assert first % MXU_TILE == 0
    bounds = [0, first] + list(range(first + POST_FFN_CHUNK, d_ff + 1, POST_FFN_CHUNK))
    return list(zip(bounds[:-1], bounds[1:]))


def _post_kernel(x_ref, attn_ref, rnn_ref, gr_ref, gm_ref, wo_ref, g2_ref, wi_ref, wf_ref, o_ref,
                 *, d_model, d_ff):
    mixed = None
    for c0 in range(0, d_model, POST_MIX_CHUNK):
        cols = slice(c0, c0 + POST_MIX_CHUNK)
        t_attn = jnp.tanh(gm_ref[:, cols])
        t_rnn = jnp.tanh(gm_ref[:, d_model + c0:d_model + c0 + POST_MIX_CHUNK])
        xq = gr_ref[:, cols]
        t_gelu = jnp.tanh(xq * (GELU_K1 + GELU_K3 * (xq * xq)))
        half_rnn = rnn_ref[:, cols] * (xq * t_gelu + xq)
        half_attn = attn_ref[:, cols].astype(F32)
        mix = (t_attn * half_attn + half_attn) + (t_rnn * half_rnn + half_rnn)
        part = jnp.dot(mix.astype(BF16), wo_ref[cols, :], preferred_element_type=F32)
        mixed = part if mixed is None else mixed + part
    h1 = x_ref[...] + mixed
    ms = jnp.mean(h1 * h1, axis=-1, keepdims=True)
    hn = (h1 * lax.rsqrt(ms + EPS) * g2_ref[...]).astype(BF16)
    out = h1
    for c0, c1 in _ffn_chunks(d_ff):
        g = jnp.dot(hn, wi_ref[:, c0:c1], preferred_element_type=F32)
        u = jnp.dot(hn, wi_ref[:, d_ff + c0:d_ff + c1], preferred_element_type=F32)
        act = (g * _sigmoid(g) * u).astype(BF16)
        out = out + jnp.dot(act, wf_ref[c0:c1, :], preferred_element_type=F32)
    o_ref[...] = out


def _post(x2d, attn, rnn, gr, gm, w_out, g2, w_ffn_in, w_ffn_out, *, tm):
    m, d = x2d.shape
    d_ff = w_ffn_out.shape[0]
    kern = functools.partial(_post_kernel, d_model=d, d_ff=d_ff)
    row = lambda w: pl.BlockSpec((tm, w), lambda i: (i, 0))
    return pl.pallas_call(
        kern,
        grid=(m // tm,),
        in_specs=[row(d), row(d), row(d), row(d), row(2 * d),
                  _resident((d, d)), _resident((1, d)), _resident((d, 2 * d_ff)), _resident((d_ff, d))],
        out_specs=row(d),
        out_shape=jax.ShapeDtypeStruct((m, d), F32),
        compiler_params=pltpu.CompilerParams(
            dimension_semantics=("arbitrary",), vmem_limit_bytes=VMEM_LIMIT_BYTES),
        name="post",
    )(x2d, attn, rnn, gr, gm, w_out, g2, w_ffn_in, w_ffn_out)


ROW_TILE = 512
ATTN_QUERY_BLOCK = 1024


def _rope_tables(seq):
    pos = jnp.arange(seq, dtype=jnp.int32)
    row = (pos // GRID_W).astype(F32)
    col = (pos % GRID_W).astype(F32)
    inv_freq = jnp.exp(-jnp.log(jnp.float32(ROPE_THETA)) * jnp.arange(ROPE_PAIRS, dtype=F32) / ROPE_PAIRS)
    ang_r = row[:, None] * inv_freq[None, :]
    ang_c = col[:, None] * inv_freq[None, :]
    cos = jnp.concatenate([jnp.cos(ang_r)] * 2 + [jnp.cos(ang_c)] * 2, axis=-1)
    sin = jnp.concatenate([-jnp.sin(ang_r), jnp.sin(ang_r), -jnp.sin(ang_c), jnp.sin(ang_c)], axis=-1)
    return cos, sin


def kernel(x, meta_tokens, norm1_g, w_in, conv_w, conv_b, rg_wa, rg_ba, rg_wx, rg_bx, rg_lambda,
           q_norm_g, k_norm_g, w_out, norm2_g, w_ffn_in, w_ffn_out):
    batch, seq, d = x.shape
    depth = norm1_g.shape[0]
    assert depth == 1, "meta-token rows are only skipped because no later layer reads them"
    q_width = N_Q_HEADS * HEAD_DIM
    kv_width = N_KV_HEADS * HEAD_DIM
    rnn_width = conv_w.shape[-1]
    n_slab = rnn_width // RNN_BLOCK
    assert meta_tokens.shape == (N_META, d) and seq % GRID_W == 0 and seq % ROW_TILE == 0

    x2d = x.reshape(batch * seq, d)
    g1 = norm1_g[0].reshape(1, d)
    gate_width = w_in.shape[-1] - q_width - 2 * kv_width - 2 * rnn_width
    col_scale = jnp.concatenate([jnp.ones((q_width + 2 * kv_width + rnn_width,), F32),
                                 jnp.full((rnn_width,), GELU_INPUT_SCALE, F32),
                                 jnp.full((gate_width,), GATE_INPUT_SCALE, F32)])
    w_in_b = (w_in[0] * col_scale).astype(BF16)
    qg = q_norm_g[0].reshape(1, HEAD_DIM)
    kg = k_norm_g[0].reshape(1, HEAD_DIM)
    cos, sin = _rope_tables(seq)
    cos_meta = jnp.ones((N_META, HEAD_DIM), F32)
    sin_meta = jnp.zeros((N_META, HEAD_DIM), F32)

    proj = functools.partial(_proj, q_width=q_width, kv_width=kv_width, rnn_width=rnn_width)
    q, k, vt, xr, gr, gm = proj(x2d, g1, w_in_b, qg, kg, cos, sin, tm=ROW_TILE)
    _, k_meta, vt_meta, xr_meta, _, _ = proj(
        meta_tokens.astype(x.dtype), g1, w_in_b, qg, kg, cos_meta, sin_meta, tm=N_META)

    score_bound = (HEAD_DIM * HEAD_DIM ** -0.5 * LOG2_E) * jnp.max(jnp.abs(qg)) * jnp.max(jnp.abs(kg))
    raw_ok = (score_bound <= ATTN_MAX_RAW_SCORE).astype(jnp.int32).reshape(1)

    w4 = jnp.concatenate([rg_wa[0, 0], rg_wx[0, 0], rg_wa[0, 1], rg_wx[0, 1]], axis=-1)
    b4 = 0.5 * jnp.stack([rg_ba[0, 0], rg_bx[0, 0], rg_ba[0, 1], rg_bx[0, 1]], axis=0)
    b4 = b4.reshape(4, n_slab, RNN_BLOCK).transpose(1, 0, 2).reshape(n_slab, 1, 4 * RNN_BLOCK)
    b4_hi = b4.astype(BF16)
    b4_lo = (b4 - b4_hi.astype(F32)).astype(BF16)
    w4_pad = jnp.zeros((n_slab, RNN_BLOCK - GATE_BIAS_ROWS, 4 * RNN_BLOCK), BF16)
    w4 = jnp.concatenate([w4.astype(BF16), b4_hi, b4_lo, w4_pad], axis=1)
    attn = _attn(raw_ok, q, k, vt, k_meta, vt_meta, batch=batch, seq=seq, bq=ATTN_QUERY_BLOCK)
    rnn = _rnn(xr, xr_meta, 0.5 * conv_w[0], 0.5 * conv_b[0].reshape(1, rnn_width), w4, rg_lambda[0],
               batch=batch, seq=seq)

    out = _post(x2d, attn, rnn, gr, gm, w_out[0].astype(BF16), norm2_g[0].reshape(1, d),
                w_ffn_in[0].astype(BF16), w_ffn_out[0].astype(BF16), tm=ROW_TILE)
    return out.reshape(batch, seq, d)
```

```python
import functools

import numpy as np

import jax
import jax.numpy as jnp
from jax import lax
from jax.experimental import pallas as pl
from jax.experimental.pallas import tpu as pltpu

N_META = 16
GRID_W = 64
HEAD_DIM = 128
N_Q_HEADS = 8
N_KV_HEADS = 2
Q_GROUP = N_Q_HEADS // N_KV_HEADS
RNN_BLOCK = 128
CONV_W = 4
CONV_PAD_L = CONV_W // 2
RG_C = 8.0
ROPE_THETA = 10000.0
ROPE_PAIRS = HEAD_DIM // 4
EPS = 1e-6
LOG2_E = 1.4426950408889634
F32_TINY = 1.1754943508222875e-38

GATE_INPUT_SCALE = 0.5
GELU_INPUT_SCALE = 0.25
ATTN_OUT_SCALE = 0.5
GELU_K1 = 0.7978845608028654 / GELU_INPUT_SCALE
GELU_K3 = 0.7978845608028654 * 0.044715 / GELU_INPUT_SCALE ** 3

LANES = 128
SUBLANES = 8
VMEM_LIMIT_BYTES = 56 * 1024 * 1024

BF16 = jnp.bfloat16
F32 = jnp.float32


def _sigmoid(x):
    return 0.5 * jnp.tanh(0.5 * x) + 0.5


def _resident(shape):
    nd = len(shape)
    return pl.BlockSpec(shape, lambda *_: (0,) * nd, pipeline_mode=pl.Buffered(1))


def _swap_halves(x):
    lane = lax.broadcasted_iota(jnp.int32, x.shape, 1)
    lo = (lane % (2 * ROPE_PAIRS)) < ROPE_PAIRS
    return jnp.where(lo, pltpu.roll(x, LANES - ROPE_PAIRS, 1), pltpu.roll(x, ROPE_PAIRS, 1))


def _head_norm(xh, g):
    ms = jnp.mean(xh * xh, axis=-1, keepdims=True)
    return xh * lax.rsqrt(ms + EPS) * g


def _head_norm_rope(xh, g, cos, sin):
    xn = _head_norm(xh, g)
    return xn * cos + _swap_halves(xn) * sin


def _proj_kernel(x_ref, meta_ref, g1_ref, w_ref, qg_ref, kg_ref, cos_ref, sin_ref,
                 q_ref, k_ref, vt_ref, xr_ref, gr_ref, gm_ref, km_ref, vtm_ref, xrm_ref,
                 *, d_model, q_width, kv_width, rnn_width):
    x = x_ref[...]
    ms = jnp.mean(x * x, axis=-1, keepdims=True)
    xn = (x * lax.rsqrt(ms + EPS) * g1_ref[...]).astype(BF16)

    cos = cos_ref[...]
    sin = sin_ref[...]
    scale = HEAD_DIM ** -0.5 * LOG2_E

    o_k = q_width
    o_v = o_k + kv_width
    o_xr = o_v + kv_width
    o_gr = o_xr + rnn_width
    o_gm = o_gr + rnn_width

    project = lambda c0, c1: jnp.dot(xn, w_ref[:, c0:c1], preferred_element_type=F32)

    pq = project(0, o_k)
    for h in range(q_width // HEAD_DIM):
        sl = slice(h * HEAD_DIM, (h + 1) * HEAD_DIM)
        qh = _head_norm_rope(pq[:, sl], qg_ref[...], cos, sin)
        q_ref[:, sl] = (qh * scale).astype(BF16)

    pkv = project(o_k, o_xr)
    for h in range(kv_width // HEAD_DIM):
        sl = slice(h * HEAD_DIM, (h + 1) * HEAD_DIM)
        kh = _head_norm_rope(pkv[:, sl], kg_ref[...], cos, sin)
        k_ref[:, sl] = kh.astype(BF16)
    vt_ref[...] = pkv[:, kv_width:].T.astype(BF16)

    pxr = project(o_xr, o_gr)
    for n in range(rnn_width // RNN_BLOCK):
        xr_ref[n] = pxr[:, n * RNN_BLOCK:(n + 1) * RNN_BLOCK]

    gr_ref[...] = project(o_gr, o_gm)
    gm_ref[...] = project(o_gm, w_ref.shape[1])

    @pl.when(pl.program_id(0) == 0)
    def _meta_tokens():
        xm = meta_ref[...]
        msm = jnp.mean(xm * xm, axis=-1, keepdims=True)
        xnm = (xm * lax.rsqrt(msm + EPS) * g1_ref[...]).astype(BF16)
        pm = jnp.dot(xnm, w_ref[:, o_k:o_gr], preferred_element_type=F32)
        for h in range(kv_width // HEAD_DIM):
            sl = slice(h * HEAD_DIM, (h + 1) * HEAD_DIM)
            km_ref[:, sl] = _head_norm(pm[:, sl], kg_ref[...]).astype(BF16)
        vtm_ref[...] = pm[:, kv_width:2 * kv_width].T.astype(BF16)
        for n in range(rnn_width // RNN_BLOCK):
            xrm_ref[n] = pm[:, 2 * kv_width + n * RNN_BLOCK:2 * kv_width + (n + 1) * RNN_BLOCK]


def _proj(x2d, meta, g1, w_in, qg, kg, cos, sin, *, tm, q_width, kv_width, rnn_width):
    m, d = x2d.shape
    in_width = w_in.shape[1]
    gm_width = in_width - q_width - 2 * kv_width - 2 * rnn_width
    n_tab = cos.shape[0] // tm
    n_slab = rnn_width // RNN_BLOCK
    kern = functools.partial(_proj_kernel, d_model=d, q_width=q_width, kv_width=kv_width,
                             rnn_width=rnn_width)
    return pl.pallas_call(
        kern,
        grid=(m // tm,),
        in_specs=[
            pl.BlockSpec((tm, d), lambda i: (i, 0)),
            _resident((N_META, d)),
            _resident((1, d)),
            _resident((d, in_width)),
            _resident((1, HEAD_DIM)),
            _resident((1, HEAD_DIM)),
            pl.BlockSpec((tm, HEAD_DIM), lambda i: (i % n_tab, 0)),
            pl.BlockSpec((tm, HEAD_DIM), lambda i: (i % n_tab, 0)),
        ],
        out_specs=[
            pl.BlockSpec((tm, q_width), lambda i: (i, 0)),
            pl.BlockSpec((tm, kv_width), lambda i: (i, 0)),
            pl.BlockSpec((kv_width, tm), lambda i: (0, i)),
            pl.BlockSpec((n_slab, tm, RNN_BLOCK), lambda i: (0, i, 0)),
            pl.BlockSpec((tm, rnn_width), lambda i: (i, 0)),
            pl.BlockSpec((tm, gm_width), lambda i: (i, 0)),
            pl.BlockSpec((N_META, kv_width), lambda i: (0, 0)),
            pl.BlockSpec((kv_width, N_META), lambda i: (0, 0)),
            pl.BlockSpec((n_slab, N_META, RNN_BLOCK), lambda i: (0, 0, 0)),
        ],
        out_shape=[
            jax.ShapeDtypeStruct((m, q_width), BF16),
            jax.ShapeDtypeStruct((m, kv_width), BF16),
            jax.ShapeDtypeStruct((kv_width, m), BF16),
            jax.ShapeDtypeStruct((n_slab, m, RNN_BLOCK), F32),
            jax.ShapeDtypeStruct((m, rnn_width), F32),
            jax.ShapeDtypeStruct((m, gm_width), F32),
            jax.ShapeDtypeStruct((N_META, kv_width), BF16),
            jax.ShapeDtypeStruct((kv_width, N_META), BF16),
            jax.ShapeDtypeStruct((n_slab, N_META, RNN_BLOCK), F32),
        ],
        compiler_params=pltpu.CompilerParams(
            dimension_semantics=("arbitrary",), vmem_limit_bytes=VMEM_LIMIT_BYTES),
        name="proj",
    )(x2d, meta, g1, w_in, qg, kg, cos, sin)


ATTN_KEY_CHUNK = 1024
ATTN_MAX_RAW_SCORE = 50.0

SCAN_SEGMENTS = 48
GATE_BIAS_ROWS = 2
RNN_SLABS_PER_STEP = 4


def _rnn_stages(xr_ref, xm_ref, cw_ref, cb_ref, w4_ref, lam_ref, o_ref,
                xs_ref, a_refs, u_refs, h_refs, *, seq):
    t_all = N_META + seq
    seg = t_all // SCAN_SEGMENTS
    n_vreg = SCAN_SEGMENTS // SUBLANES
    pad = SUBLANES
    chains = [(d, v) for d in range(2) for v in range(n_vreg)]
    live = {}

    def rows(d, jj, v):
        j = jj if d == 0 else seg - 1 - jj
        return pl.ds(j + v * SUBLANES * seg, SUBLANES, stride=seg)

    def conv():
        xs_ref[0:pad, :] = jnp.zeros((pad, LANES), F32)
        xs_ref[pad:pad + N_META, :] = xm_ref[0]
        xs_ref[pad + N_META:pad + t_all, :] = xr_ref[0]
        xs_ref[pad + t_all:pad + t_all + pad, :] = jnp.zeros((pad, LANES), F32)
        xh = cb_ref[...]
        for j in range(CONV_W):
            off = pad + j - CONV_PAD_L
            xh = xh + xs_ref[off:off + t_all, :] * cw_ref[j:j + 1, :]
        live["xh"] = xh

    def gate_matmul():
        lane = lax.broadcasted_iota(jnp.int32, (t_all, LANES), 1)
        ones_cols = jnp.where(lane < GATE_BIAS_ROWS, 1.0, 0.0).astype(BF16)
        live["zh"] = jnp.dot(jnp.concatenate([live["xh"].astype(BF16), ones_cols], axis=1), w4_ref[0],
                             preferred_element_type=F32)

    def decay_and_input(d):
        zh, xh = live["zh"], live["xh"]
        t_r = jnp.tanh(zh[:, (2 * d) * LANES:(2 * d + 1) * LANES])
        t_i = jnp.tanh(zh[:, (2 * d + 1) * LANES:(2 * d + 2) * LANES])
        c_half = (-0.5 * RG_C) * jax.nn.log_sigmoid(lam_ref[d:d + 1, :])
        neg_log_a = c_half * t_r + c_half
        a = jnp.exp2(neg_log_a * (-LOG2_E))
        a_refs[d][...] = a
        y = jnp.tanh(neg_log_a) * (a * a + 1.0)
        root = y * lax.rsqrt(jnp.maximum(y, F32_TINY))
        u_refs[d][...] = root * (t_i * xh + xh)

    def segment_end_states():
        carry = [(jnp.zeros((SUBLANES, LANES), F32), jnp.ones((SUBLANES, LANES), F32))] * len(chains)
        for jj in range(seg):
            new = []
            for (d, v), (h, p) in zip(chains, carry):
                av = a_refs[d][rows(d, jj, v), :]
                new.append((av * h + u_refs[d][rows(d, jj, v), :], av * p))
            carry = new
        start = {}
        for d in range(2):
            order = range(SCAN_SEGMENTS) if d == 0 else range(SCAN_SEGMENTS - 1, -1, -1)
            c = jnp.zeros((1, LANES), F32)
            cs = [None] * SCAN_SEGMENTS
            for s in order:
                cs[s] = c
                h_end, p_end = carry[chains.index((d, s // SUBLANES))]
                r = s % SUBLANES
                c = p_end[r:r + 1, :] * c + h_end[r:r + 1, :]
            for v in range(n_vreg):
                start[(d, v)] = jnp.concatenate(cs[v * SUBLANES:(v + 1) * SUBLANES], axis=0)
        live["start"] = start

    def states_and_output():
        carry = [live["start"][c] for c in chains]
        for jj in range(seg):
            new = []
            for (d, v), h in zip(chains, carry):
                h = a_refs[d][rows(d, jj, v), :] * h + u_refs[d][rows(d, jj, v), :]
                h_refs[d][rows(d, jj, v), :] = h
                new.append(h)
            carry = new
        o_ref[...] = h_refs[0][N_META:, :] + h_refs[1][N_META:, :]

    return [conv, gate_matmul, functools.partial(decay_and_input, 0),
            functools.partial(decay_and_input, 1), segment_end_states, states_and_output]


def _rnn_kernel(xr_ref, xm_ref, cw_ref, cb_ref, w4_ref, lam_ref, o_ref, *scratch, seq):
    per_slab = len(scratch) // RNN_SLABS_PER_STEP
    slab_stages = []
    for s in range(RNN_SLABS_PER_STEP):
        xs_ref, a0_ref, a1_ref, u0_ref, u1_ref, h0_ref, h1_ref = scratch[s * per_slab:(s + 1) * per_slab]
        lanes = slice(s * LANES, (s + 1) * LANES)
        slab_stages.append(_rnn_stages(
            xr_ref.at[s:s + 1], xm_ref.at[s:s + 1], cw_ref.at[:, lanes], cb_ref.at[:, lanes],
            w4_ref.at[s:s + 1], lam_ref.at[:, lanes], o_ref.at[:, lanes], xs_ref,
            (a0_ref, a1_ref), (u0_ref, u1_ref), (h0_ref, h1_ref), seq=seq))
    for stages in zip(*slab_stages):
        for stage in stages:
            stage()


def _attn_kernel(raw_ok_ref, q_ref, k_ref, vt_ref, km_ref, vtm_ref, o_ref, *, bq):
    nt = (((1,), (1,)), ((), ()))
    seq = k_ref.shape[0]
    n_chunks = seq // ATTN_KEY_CHUNK
    chunk = lambda c: slice(c * ATTN_KEY_CHUNK, (c + 1) * ATTN_KEY_CHUNK)

    def attend(running_maximum):
        q = q_ref[...]
        qs = jnp.concatenate([q[:, g * HEAD_DIM:(g + 1) * HEAD_DIM] for g in range(Q_GROUP)], axis=0)
        scores = lambda c: lax.dot_general(k_ref[chunk(c), :], qs, nt, preferred_element_type=F32)

        sm = lax.dot_general(km_ref[...], qs, nt, preferred_element_type=F32)
        m = jnp.max(sm, axis=0, keepdims=True) if running_maximum else None
        pm = jnp.exp2(sm - m) if running_maximum else jnp.exp2(sm)
        l = jnp.sum(pm, axis=0, keepdims=True)
        ot = jnp.dot(vtm_ref[...], pm.astype(BF16), preferred_element_type=F32)
        s_next = scores(0)
        for c in range(n_chunks):
            s = s_next
            if c + 1 < n_chunks:
                s_next = scores(c + 1)
            if running_maximum:
                m_new = jnp.maximum(m, jnp.max(s, axis=0, keepdims=True))
                alpha = jnp.exp2(m - m_new)
                p = jnp.exp2(s - m_new)
                l = alpha * l + jnp.sum(p, axis=0, keepdims=True)
                ot = alpha * ot
                m = m_new
            else:
                p = jnp.exp2(s)
                l = l + jnp.sum(p, axis=0, keepdims=True)
            ot = ot + jnp.dot(vt_ref[:, chunk(c)], p.astype(BF16), preferred_element_type=F32)
        ot = ot * (ATTN_OUT_SCALE / l)
        for g in range(Q_GROUP):
            o_ref[:, g * HEAD_DIM:(g + 1) * HEAD_DIM] = ot[:, g * bq:(g + 1) * bq].T.astype(BF16)

    @pl.when(raw_ok_ref[0] == 1)
    def _raw_scores():
        attend(running_maximum=False)

    @pl.when(raw_ok_ref[0] != 1)
    def _running_maximum():
        attend(running_maximum=True)


def _attn(raw_ok, q, k, vt, k_meta, vt_meta, *, batch, seq, bq):
    m = q.shape[0]
    nq = seq // bq
    gw = Q_GROUP * HEAD_DIM
    assert seq % ATTN_KEY_CHUNK == 0
    kern = functools.partial(_attn_kernel, bq=bq)
    return pl.pallas_call(
        kern,
        grid=(batch, N_KV_HEADS, nq),
        in_specs=[
            pl.BlockSpec(memory_space=pltpu.SMEM),
            pl.BlockSpec((bq, gw), lambda b, h, i: (b * nq + i, h)),
            pl.BlockSpec((seq, HEAD_DIM), lambda b, h, i: (b, h)),
            pl.BlockSpec((HEAD_DIM, seq), lambda b, h, i: (h, b)),
            pl.BlockSpec((N_META, HEAD_DIM), lambda b, h, i: (0, h)),
            pl.BlockSpec((HEAD_DIM, N_META), lambda b, h, i: (h, 0)),
        ],
        out_specs=pl.BlockSpec((bq, gw), lambda b, h, i: (b * nq + i, h)),
        out_shape=jax.ShapeDtypeStruct((m, N_Q_HEADS * HEAD_DIM), BF16),
        compiler_params=pltpu.CompilerParams(
            dimension_semantics=("arbitrary", "arbitrary", "arbitrary"),
            vmem_limit_bytes=VMEM_LIMIT_BYTES),
        name="attn",
    )(raw_ok, q, k, vt, k_meta, vt_meta)


def _rnn(xr, xr_meta, conv_w, conv_b, w4, lam, *, batch, seq):
    n_slab, m, _ = xr.shape
    t_all = N_META + seq
    assert t_all % SCAN_SEGMENTS == 0
    kern = functools.partial(_rnn_kernel, seq=seq)
    g = RNN_SLABS_PER_STEP
    assert n_slab % g == 0
    return pl.pallas_call(
        kern,
        grid=(batch, n_slab // g),
        in_specs=[
            pl.BlockSpec((g, seq, LANES), lambda b, n: (n, b, 0)),
            pl.BlockSpec((g, N_META, LANES), lambda b, n: (n, 0, 0)),
            pl.BlockSpec((CONV_W, g * LANES), lambda b, n: (0, n)),
            pl.BlockSpec((1, g * LANES), lambda b, n: (0, n)),
            pl.BlockSpec((g, 2 * LANES, 4 * LANES), lambda b, n: (n, 0, 0)),
            pl.BlockSpec((2, g * LANES), lambda b, n: (0, n)),
        ],
        out_specs=pl.BlockSpec((seq, g * LANES), lambda b, n: (b, n)),
        out_shape=jax.ShapeDtypeStruct((m, n_slab * LANES), F32),
        scratch_shapes=([pltpu.VMEM((t_all + 2 * SUBLANES, LANES), F32)]
                        + [pltpu.VMEM((t_all, LANES), F32)] * 6) * g,
        compiler_params=pltpu.CompilerParams(
            dimension_semantics=("arbitrary", "arbitrary"), vmem_limit_bytes=VMEM_LIMIT_BYTES),
        name="rnn",
    )(xr, xr_meta, conv_w, conv_b, w4, lam)


MXU_TILE = 256
POST_FFN_CHUNK = 512
POST_MIX_CHUNK = 256


def _ffn_chunks(d_ff):
    first = d_ff % POST_FFN_CHUNK or POST_FFN_CHUNK
    assert first % MXU_TILE == 0
    bounds = [0, first] + list(range(first + POST_FFN_CHUNK, d_ff + 1, POST_FFN_CHUNK))
    return list(zip(bounds[:-1], bounds[1:]))


def _post_kernel(x_ref, attn_ref, rnn_ref, gr_ref, gm_ref, wo_ref, g2_ref, wi_ref, wf_ref, o_ref,
                 *, d_model, d_ff):
    mixed = None
    for c0 in range(0, d_model, POST_MIX_CHUNK):
        cols = slice(c0, c0 + POST_MIX_CHUNK)
        t_attn = jnp.tanh(gm_ref[:, cols])
        t_rnn = jnp.tanh(gm_ref[:, d_model + c0:d_model + c0 + POST_MIX_CHUNK])
        xq = gr_ref[:, cols]
        t_gelu = jnp.tanh(xq * (GELU_K1 + GELU_K3 * (xq * xq)))
        half_rnn = rnn_ref[:, cols] * (xq * t_gelu + xq)
        half_attn = attn_ref[:, cols].astype(F32)
        mix = (t_attn * half_attn + half_attn) + (t_rnn * half_rnn + half_rnn)
        part = jnp.dot(mix.astype(BF16), wo_ref[cols, :], preferred_element_type=F32)
        mixed = part if mixed is None else mixed + part
    h1 = x_ref[...] + mixed
    ms = jnp.mean(h1 * h1, axis=-1, keepdims=True)
    hn = (h1 * lax.rsqrt(ms + EPS) * g2_ref[...]).astype(BF16)
    out = h1
    for c0, c1 in _ffn_chunks(d_ff):
        g = jnp.dot(hn, wi_ref[:, c0:c1], preferred_element_type=F32)
        u = jnp.dot(hn, wi_ref[:, d_ff + c0:d_ff + c1], preferred_element_type=F32)
        act = (g * _sigmoid(g) * u).astype(BF16)
        out = out + jnp.dot(act, wf_ref[c0:c1, :], preferred_element_type=F32)
    o_ref[...] = out


def _post(x2d, attn, rnn, gr, gm, w_out, g2, w_ffn_in, w_ffn_out, *, tm):
    m, d = x2d.shape
    d_ff = w_ffn_out.shape[0]
    kern = functools.partial(_post_kernel, d_model=d, d_ff=d_ff)
    row = lambda w: pl.BlockSpec((tm, w), lambda i: (i, 0))
    return pl.pallas_call(
        kern,
        grid=(m // tm,),
        in_specs=[row(d), row(d), row(d), row(d), row(2 * d),
                  _resident((d, d)), _resident((1, d)), _resident((d, 2 * d_ff)), _resident((d_ff, d))],
        out_specs=row(d),
        out_shape=jax.ShapeDtypeStruct((m, d), F32),
        compiler_params=pltpu.CompilerParams(
            dimension_semantics=("arbitrary",), vmem_limit_bytes=VMEM_LIMIT_BYTES),
        name="post",
    )(x2d, attn, rnn, gr, gm, w_out, g2, w_ffn_in, w_ffn_out)


ROW_TILE = 512
ATTN_QUERY_BLOCK = 1024


def _rope_tables(seq):
    f32 = np.float32
    pos = np.arange(seq)
    row = (pos // GRID_W).astype(f32)
    col = (pos % GRID_W).astype(f32)
    inv_freq = np.exp(-np.log(f32(ROPE_THETA)) * np.arange(ROPE_PAIRS, dtype=f32) / f32(ROPE_PAIRS)).astype(f32)
    ang_r = row[:, None] * inv_freq[None, :]
    ang_c = col[:, None] * inv_freq[None, :]
    cos = np.concatenate([np.cos(ang_r)] * 2 + [np.cos(ang_c)] * 2, axis=-1).astype(f32)
    sin = np.concatenate([-np.sin(ang_r), np.sin(ang_r), -np.sin(ang_c), np.sin(ang_c)], axis=-1).astype(f32)
    return jnp.asarray(cos), jnp.asarray(sin)


def kernel(x, meta_tokens, norm1_g, w_in, conv_w, conv_b, rg_wa, rg_ba, rg_wx, rg_bx, rg_lambda,
           q_norm_g, k_norm_g, w_out, norm2_g, w_ffn_in, w_ffn_out):
    batch, seq, d = x.shape
    depth = norm1_g.shape[0]
    assert depth == 1, "meta-token rows are only skipped because no later layer reads them"
    q_width = N_Q_HEADS * HEAD_DIM
    kv_width = N_KV_HEADS * HEAD_DIM
    rnn_width = conv_w.shape[-1]
    n_slab = rnn_width // RNN_BLOCK
    assert meta_tokens.shape == (N_META, d) and seq % GRID_W == 0 and seq % ROW_TILE == 0

    x2d = x.reshape(batch * seq, d)
    g1 = norm1_g[0].reshape(1, d)
    gate_width = w_in.shape[-1] - q_width - 2 * kv_width - 2 * rnn_width
    col_scale = jnp.concatenate([jnp.ones((q_width + 2 * kv_width + rnn_width,), F32),
                                 jnp.full((rnn_width,), GELU_INPUT_SCALE, F32),
                                 jnp.full((gate_width,), GATE_INPUT_SCALE, F32)])
    w_in_b = (w_in[0] * col_scale).astype(BF16)
    qg = q_norm_g[0].reshape(1, HEAD_DIM)
    kg = k_norm_g[0].reshape(1, HEAD_DIM)
    cos, sin = _rope_tables(seq)
    q, k, vt, xr, gr, gm, k_meta, vt_meta, xr_meta = _proj(
        x2d, meta_tokens.astype(x.dtype), g1, w_in_b, qg, kg, cos, sin, tm=ROW_TILE,
        q_width=q_width, kv_width=kv_width, rnn_width=rnn_width)

    score_bound = (HEAD_DIM * HEAD_DIM ** -0.5 * LOG2_E) * jnp.max(jnp.abs(qg)) * jnp.max(jnp.abs(kg))
    raw_ok = (score_bound <= ATTN_MAX_RAW_SCORE).astype(jnp.int32).reshape(1)

    w4 = jnp.concatenate([rg_wa[0, 0], rg_wx[0, 0], rg_wa[0, 1], rg_wx[0, 1]], axis=-1)
    b4 = 0.5 * jnp.stack([rg_ba[0, 0], rg_bx[0, 0], rg_ba[0, 1], rg_bx[0, 1]], axis=0)
    b4 = b4.reshape(4, n_slab, RNN_BLOCK).transpose(1, 0, 2).reshape(n_slab, 1, 4 * RNN_BLOCK)
    b4_hi = b4.astype(BF16)
    b4_lo = (b4 - b4_hi.astype(F32)).astype(BF16)
    w4_pad = jnp.zeros((n_slab, RNN_BLOCK - GATE_BIAS_ROWS, 4 * RNN_BLOCK), BF16)
    w4 = jnp.concatenate([w4.astype(BF16), b4_hi, b4_lo, w4_pad], axis=1)
    attn = _attn(raw_ok, q, k, vt, k_meta, vt_meta, batch=batch, seq=seq, bq=ATTN_QUERY_BLOCK)
    rnn = _rnn(xr, xr_meta, 0.5 * conv_w[0], 0.5 * conv_b[0].reshape(1, rnn_width), w4, rg_lambda[0],
               batch=batch, seq=seq)

    out = _post(x2d, attn, rnn, gr, gm, w_out[0].astype(BF16), norm2_g[0].reshape(1, d),
                w_ffn_in[0].astype(BF16), w_ffn_out[0].astype(BF16), tm=ROW_TILE)
    return out.reshape(batch, seq, d)
```
